```python
import jax, jax.numpy as jnp
from jax import lax
import numpy as np

D_MODEL = 1024
BATCH = 2
SEQ = 16384
DEPTH = 1
DEC_BATCH = 16
DEC_SEQ = 16
PAST_LEN = 2048

CHUNK = 64
N_META = 16
N_HEADS = 16
N_KV_HEADS = 2
HEAD_DIM = 64
HALF_DIM = HEAD_DIM // 2
Q_PER_KV = N_HEADS // N_KV_HEADS
ATTN_WIDTH = N_HEADS * HEAD_DIM
KV_WIDTH = N_KV_HEADS * HEAD_DIM
WINDOW = 128
WINDOW_CHUNKS = WINDOW // CHUNK
ROPE_THETA = 10000.0
LRU_WIDTH = D_MODEL
LRU_BLOCKS = 16
LRU_BLOCK_DIM = LRU_WIDTH // LRU_BLOCKS
LRU_C = 8.0
CONV_WIDTH = 4
N_GROUPS = 4
EXPERTS_PER_GROUP = 4
N_EXPERTS = N_GROUPS * EXPERTS_PER_GROUP
TOP_K_IN_GROUP = 2
D_EXPERT = 256
RMS_EPS = 1e-6
NEG_INF = -1e30
Q_END = ATTN_WIDTH
K_END = Q_END + KV_WIDTH
V_END = K_END + KV_WIDTH
XR_END = V_END + LRU_WIDTH
GR_END = XR_END + LRU_WIDTH
GA_END = GR_END + D_MODEL
IN_WIDTH = GA_END + D_MODEL

kernel_name = "hybrid_swa_sink_rglru_hmoe_stream_step"


def rmsnorm(x, g):
    xf = x.astype(jnp.float32)
    xf = xf * lax.rsqrt(jnp.mean(xf * xf, axis=-1, keepdims=True) + RMS_EPS)
    return xf.astype(x.dtype) * g


def rope(x, pos):
    inv_freq = ROPE_THETA ** (-jnp.arange(0, HEAD_DIM, 2, dtype=jnp.float32) / HEAD_DIM)
    ang = pos.astype(jnp.float32)[:, None] * inv_freq[None, :]
    cos = jnp.cos(ang)[:, None, :]
    sin = jnp.sin(ang)[:, None, :]
    xf = x.astype(jnp.float32)
    x1, x2 = xf[..., :HALF_DIM], xf[..., HALF_DIM:]
    return jnp.concatenate([x1 * cos - x2 * sin, x2 * cos + x1 * sin], axis=-1).astype(x.dtype)


def split_proj(p):
    return jnp.split(p, [Q_END, K_END, V_END, XR_END, GR_END, GA_END], axis=-1)


def attend(q, k, v, sinks, mask=None):
    s = jnp.einsum('...qkgd,...skd->...kgqs', q, k).astype(jnp.float32) * (HEAD_DIM ** -0.5)
    if mask is not None:
        s = jnp.where(mask, s, NEG_INF)
    sink = jnp.broadcast_to(sinks.astype(jnp.float32).reshape(N_KV_HEADS, Q_PER_KV, 1, 1), s.shape[:-1] + (1,))
    p = jax.nn.softmax(jnp.concatenate([s, sink], axis=-1), axis=-1)[..., :-1]
    return jnp.einsum('...kgqs,...skd->...qkgd', p.astype(v.dtype), v)


def causal_dwconv(x_hist, w, b):
    t = x_hist.shape[-2] - (CONV_WIDTH - 1)
    out = b
    for tap in range(CONV_WIDTH):
        out = out + x_hist[..., tap:tap + t, :] * w[tap]
    return out


def _lru_combine(left, right):
    a1, b1 = left
    a2, b2 = right
    return a1 * a2, a2 * b1 + b2


def rg_lru(x, h0, lw):
    xf = x.astype(jnp.float32)
    xb = xf.reshape(xf.shape[:-1] + (LRU_BLOCKS, LRU_BLOCK_DIM))
    r = jax.nn.sigmoid(jnp.einsum('...nd,nde->...ne', xb, lw['w_r']).reshape(xf.shape) + lw['b_r'])
    i = jax.nn.sigmoid(jnp.einsum('...nd,nde->...ne', xb, lw['w_i']).reshape(xf.shape) + lw['b_i'])
    log_a = -LRU_C * r * jax.nn.softplus(-lw['lam'].astype(jnp.float32))
    a = jnp.exp(log_a)
    b = jnp.sqrt(-jnp.expm1(2.0 * log_a)) * (i * xf)
    b = b.at[:, 0].add(a[:, 0] * h0.astype(jnp.float32))
    _, h = lax.associative_scan(_lru_combine, (a, b), axis=1)
    return h.astype(x.dtype)


def merge(ya, yb, ga, gb, lw):
    return (jax.nn.sigmoid(ga) * (ya @ lw['w_attn_o']) + jax.nn.sigmoid(gb) * (yb @ lw['w_rec_o'])) @ lw['w_out']


def hier_moe(x, lw):
    t = x.reshape(-1, D_MODEL)
    g_logits = (t @ lw['w_group']).astype(jnp.float32)
    g_prob = jax.nn.softmax(g_logits, axis=-1)
    g_idx = jnp.argmax(g_logits, axis=-1)
    g_w = jnp.take_along_axis(g_prob, g_idx[:, None], axis=-1)[:, 0]
    e_logits = (t @ lw['w_router']).astype(jnp.float32).reshape(-1, N_GROUPS, EXPERTS_PER_GROUP)
    e_sel = jnp.take_along_axis(e_logits, g_idx[:, None, None], axis=1)[:, 0]
    top_v, top_i = lax.top_k(e_sel, TOP_K_IN_GROUP)
    top_w = jax.nn.softmax(top_v, axis=-1) * g_w[:, None]
    expert_id = g_idx[:, None] * EXPERTS_PER_GROUP + top_i
    combine = jnp.sum(jax.nn.one_hot(expert_id, N_EXPERTS, dtype=jnp.float32) * top_w[..., None], axis=1)
    hg = jnp.einsum('td,edf->tef', t, lw['w_gate'])
    hu = jnp.einsum('td,edf->tef', t, lw['w_up'])
    act = jax.nn.silu(hg) * hu * combine[..., None].astype(t.dtype)
    return jnp.einsum('tef,efd->td', act, lw['w_down']).reshape(x.shape)


def channel_block(h, lw):
    return h + hier_moe(rmsnorm(h, lw['g_ffn']), lw)


def prompt_layer(x, k_m, v_m, xr_m, lw):
    bsz, seq, _ = x.shape
    nc = seq // CHUNK
    q, k, v, xr, gr, ga, gb = split_proj(rmsnorm(x, lw['g_mix']) @ lw['w_in'])
    pos = N_META + jnp.arange(seq, dtype=jnp.int32)
    q = rope(q.reshape(bsz, seq, N_HEADS, HEAD_DIM), pos).reshape(bsz, nc, CHUNK, N_KV_HEADS, Q_PER_KV, HEAD_DIM)
    k = rope(k.reshape(bsz, seq, N_KV_HEADS, HEAD_DIM), pos)
    v = v.reshape(bsz, seq, N_KV_HEADS, HEAD_DIM)

    def band(t):
        tp = jnp.pad(t.reshape(bsz, nc, CHUNK, N_KV_HEADS, HEAD_DIM), ((0, 0), (WINDOW_CHUNKS, 0), (0, 0), (0, 0), (0, 0)))
        return jnp.concatenate([tp[:, o:o + nc] for o in range(WINDOW_CHUNKS + 1)], axis=2)

    def meta_b(t):
        return jnp.broadcast_to(t, (bsz, nc) + t.shape)

    keys = jnp.concatenate([meta_b(k_m), band(k)], axis=2)
    vals = jnp.concatenate([meta_b(v_m), band(v)], axis=2)
    key_chunk = jnp.arange(nc)[:, None] - WINDOW_CHUNKS + jnp.arange(WINDOW_CHUNKS + 1)[None, :]
    band_ok = jnp.repeat(key_chunk >= 0, CHUNK, axis=1)
    mask = jnp.concatenate([jnp.ones((nc, N_META), dtype=bool), band_ok], axis=1)[:, None, None, None, :]
    ya = attend(q, keys, vals, lw['sinks'], mask).reshape(bsz, seq, ATTN_WIDTH)

    xr_all = jnp.concatenate([jnp.broadcast_to(xr_m, (bsz, N_META, LRU_WIDTH)), xr], axis=1)
    conv_in = jnp.pad(xr_all, ((0, 0), (CONV_WIDTH - 1, 0), (0, 0)))
    h = rg_lru(causal_dwconv(conv_in, lw['conv_w'], lw['conv_b']), jnp.zeros((bsz, LRU_WIDTH), x.dtype), lw)
    yb = h[:, N_META:] * jax.nn.gelu(gr)
    x = channel_block(x + merge(ya, yb, ga, gb, lw), lw)
    return x, (k[:, -WINDOW:], v[:, -WINDOW:], xr_all[:, -(CONV_WIDTH - 1):], h[:, -1])


def sample_layer(x, cache_k, cache_v, conv_state, h0, k_m, v_m, lw):
    bsz, seq, _ = x.shape
    q, k, v, xr, gr, ga, gb = split_proj(rmsnorm(x, lw['g_mix']) @ lw['w_in'])
    pos = N_META + PAST_LEN + jnp.arange(seq, dtype=jnp.int32)
    q = rope(q.reshape(bsz, seq, N_HEADS, HEAD_DIM), pos).reshape(bsz, seq, N_KV_HEADS, Q_PER_KV, HEAD_DIM)
    k = rope(k.reshape(bsz, seq, N_KV_HEADS, HEAD_DIM), pos)
    v = v.reshape(bsz, seq, N_KV_HEADS, HEAD_DIM)
    k_all = jnp.concatenate([cache_k, k], axis=1)
    v_all = jnp.concatenate([cache_v, v], axis=1)
    keys = jnp.concatenate([jnp.broadcast_to(k_m, (bsz,) + k_m.shape), k_all], axis=1)
    vals = jnp.concatenate([jnp.broadcast_to(v_m, (bsz,) + v_m.shape), v_all], axis=1)
    ya = attend(q, keys, vals, lw['sinks']).reshape(bsz, seq, ATTN_WIDTH)

    conv_in = jnp.concatenate([conv_state, xr], axis=1)
    h = rg_lru(causal_dwconv(conv_in, lw['conv_w'], lw['conv_b']), h0, lw)
    yb = h * jax.nn.gelu(gr)
    x = channel_block(x + merge(ya, yb, ga, gb, lw), lw)
    return x, (k_all[:, seq:], v_all[:, seq:], conv_in[:, -(CONV_WIDTH - 1):], h[:, -1])


def meta_update(m, qm, km, vm, xrm, grm, gam, gbm, lw):
    qm = rope(qm.reshape(N_META, N_HEADS, HEAD_DIM), jnp.arange(N_META, dtype=jnp.int32)).reshape(N_META, N_KV_HEADS, Q_PER_KV, HEAD_DIM)
    ya = attend(qm, km, vm, lw['sinks']).reshape(N_META, ATTN_WIDTH)
    conv = causal_dwconv(jnp.pad(xrm, ((CONV_WIDTH - 1, 0), (0, 0))), lw['conv_w'], lw['conv_b'])
    h = rg_lru(conv[None], jnp.zeros((1, LRU_WIDTH), m.dtype), lw)[0]
    return channel_block(m + merge(ya, h * jax.nn.gelu(grm), gam, gbm, lw), lw)


def setup_inputs(seed: int = 0) -> dict:
    key = jax.random.key(seed)
    ks = jax.random.split(key, 32)

    def nrm(k, shape, scale):
        return jax.random.normal(k, shape, jnp.float32) * scale

    u = jax.random.uniform(ks[14], (DEPTH, LRU_WIDTH), jnp.float32, minval=0.9, maxval=0.999)
    s = u ** (1.0 / LRU_C)
    return {
        "x_prompt": nrm(ks[0], (BATCH, SEQ, D_MODEL), 1.0),
        "x_sample": nrm(ks[1], (DEC_BATCH, DEC_SEQ, D_MODEL), 1.0),
        "cache_attn_k": nrm(ks[2], (DEPTH, DEC_BATCH, WINDOW, N_KV_HEADS, HEAD_DIM), 1.0),
        "cache_attn_v": nrm(ks[3], (DEPTH, DEC_BATCH, WINDOW, N_KV_HEADS, HEAD_DIM), 1.0),
        "state_conv": nrm(ks[4], (DEPTH, DEC_BATCH, CONV_WIDTH - 1, LRU_WIDTH), 1.0),
        "state_lru": nrm(ks[5], (DEPTH, DEC_BATCH, LRU_WIDTH), 0.5),
        "meta_tokens": nrm(ks[6], (N_META, D_MODEL), 1.0),
        "norm_mix_g": 1.0 + nrm(ks[7], (DEPTH, D_MODEL), 0.02),
        "w_in": nrm(ks[8], (DEPTH, D_MODEL, IN_WIDTH), D_MODEL ** -0.5),
        "attn_sinks": nrm(ks[9], (DEPTH, N_HEADS), 0.5),
        "w_attn_o": nrm(ks[10], (DEPTH, ATTN_WIDTH, D_MODEL), ATTN_WIDTH ** -0.5),
        "conv_w": nrm(ks[11], (DEPTH, CONV_WIDTH, LRU_WIDTH), CONV_WIDTH ** -0.5),
        "conv_b": nrm(ks[12], (DEPTH, LRU_WIDTH), 0.02),
        "w_rg_r": nrm(ks[13], (DEPTH, LRU_BLOCKS, LRU_BLOCK_DIM, LRU_BLOCK_DIM), LRU_BLOCK_DIM ** -0.5),
        "b_rg_r": nrm(ks[15], (DEPTH, LRU_WIDTH), 0.02),
        "w_rg_i": nrm(ks[16], (DEPTH, LRU_BLOCKS, LRU_BLOCK_DIM, LRU_BLOCK_DIM), LRU_BLOCK_DIM ** -0.5),
        "b_rg_i": nrm(ks[17], (DEPTH, LRU_WIDTH), 0.02),
        "lru_lambda": jnp.log(s) - jnp.log1p(-s),
        "w_rec_o": nrm(ks[18], (DEPTH, LRU_WIDTH, D_MODEL), LRU_WIDTH ** -0.5),
        "w_out": nrm(ks[19], (DEPTH, D_MODEL, D_MODEL), D_MODEL ** -0.5),
        "norm_ffn_g": 1.0 + nrm(ks[20], (DEPTH, D_MODEL), 0.02),
        "w_group_router": nrm(ks[21], (DEPTH, D_MODEL, N_GROUPS), D_MODEL ** -0.5),
        "w_expert_router": nrm(ks[22], (DEPTH, D_MODEL, N_EXPERTS), D_MODEL ** -0.5),
        "w_gate_e": nrm(ks[23], (DEPTH, N_EXPERTS, D_MODEL, D_EXPERT), D_MODEL ** -0.5),
        "w_up_e": nrm(ks[24], (DEPTH, N_EXPERTS, D_MODEL, D_EXPERT), D_MODEL ** -0.5),
        "w_down_e": nrm(ks[25], (DEPTH, N_EXPERTS, D_EXPERT, D_MODEL), D_EXPERT ** -0.5),
        "norm_final_g": 1.0 + nrm(ks[26], (D_MODEL,), 0.02),
    }


def reference(x_prompt, x_sample, cache_attn_k, cache_attn_v, state_conv, state_lru, meta_tokens,
              norm_mix_g, w_in, attn_sinks, w_attn_o, conv_w, conv_b, w_rg_r, b_rg_r, w_rg_i, b_rg_i,
              lru_lambda, w_rec_o, w_out, norm_ffn_g, w_group_router, w_expert_router, w_gate_e, w_up_e,
              w_down_e, norm_final_g):
    xp, xs, m = x_prompt, x_sample, meta_tokens
    kp_l, vp_l, cp_l, hp_l, ks_l, vs_l, cs_l, hs_l = [], [], [], [], [], [], [], []
    for layer in range(DEPTH):
        lw = dict(g_mix=norm_mix_g[layer], w_in=w_in[layer], sinks=attn_sinks[layer], w_attn_o=w_attn_o[layer],
                  conv_w=conv_w[layer], conv_b=conv_b[layer], w_r=w_rg_r[layer], b_r=b_rg_r[layer],
                  w_i=w_rg_i[layer], b_i=b_rg_i[layer], lam=lru_lambda[layer], w_rec_o=w_rec_o[layer],
                  w_out=w_out[layer], g_ffn=norm_ffn_g[layer], w_group=w_group_router[layer],
                  w_router=w_expert_router[layer], w_gate=w_gate_e[layer], w_up=w_up_e[layer], w_down=w_down_e[layer])
        qm, km, vm, xrm, grm, gam, gbm = split_proj(rmsnorm(m, lw['g_mix']) @ lw['w_in'])
        km = rope(km.reshape(N_META, N_KV_HEADS, HEAD_DIM), jnp.arange(N_META, dtype=jnp.int32))
        vm = vm.reshape(N_META, N_KV_HEADS, HEAD_DIM)

        xp, (kp, vp, cp, hp) = prompt_layer(xp, km, vm, xrm, lw)
        xs, (kn, vn, cn, hn) = sample_layer(xs, cache_attn_k[layer], cache_attn_v[layer], state_conv[layer],
                                            state_lru[layer], km, vm, lw)
        kp_l.append(kp); vp_l.append(vp); cp_l.append(cp); hp_l.append(hp)
        ks_l.append(kn); vs_l.append(vn); cs_l.append(cn); hs_l.append(hn)
        if layer < DEPTH - 1:
            m = meta_update(m, qm, km, vm, xrm, grm, gam, gbm, lw)

    y_prompt = rmsnorm(xp, norm_final_g)
    y_sample = rmsnorm(xs, norm_final_g)
    new_k_prompt = jnp.stack(kp_l, axis=0)
    new_v_prompt = jnp.stack(vp_l, axis=0)
    new_conv_prompt = jnp.stack(cp_l, axis=0)
    new_lru_prompt = jnp.stack(hp_l, axis=0)
    new_k_sample = jnp.stack(ks_l, axis=0)
    new_v_sample = jnp.stack(vs_l, axis=0)
    new_conv_sample = jnp.stack(cs_l, axis=0)
    new_lru_sample = jnp.stack(hs_l, axis=0)
    return (y_prompt, y_sample, new_k_prompt, new_v_prompt, new_conv_prompt, new_lru_prompt,
            new_k_sample, new_v_sample, new_conv_sample, new_lru_sample)
```

```python
import functools

import jax
import jax.numpy as jnp
from jax import lax
from jax.experimental import pallas as pl
from jax.experimental.pallas import tpu as pltpu

D_MODEL = 1024
CHUNK = 64
N_META = 16
N_HEADS = 16
N_KV_HEADS = 2
HEAD_DIM = 64
HALF_DIM = HEAD_DIM // 2
Q_PER_KV = N_HEADS // N_KV_HEADS
ATTN_WIDTH = N_HEADS * HEAD_DIM
KV_WIDTH = N_KV_HEADS * HEAD_DIM
WINDOW = 128
ROPE_THETA = 10000.0
LRU_WIDTH = D_MODEL
LRU_BLOCKS = 16
LRU_BLOCK_DIM = LRU_WIDTH // LRU_BLOCKS
LRU_C = 8.0
CONV_WIDTH = 4
N_GROUPS = 4
EXPERTS_PER_GROUP = 4
N_EXPERTS = N_GROUPS * EXPERTS_PER_GROUP
D_EXPERT = 256
RMS_EPS = 1e-6
NEG_INF = -1e30
Q_END = ATTN_WIDTH
K_END = Q_END + KV_WIDTH
V_END = K_END + KV_WIDTH
XR_END = V_END + LRU_WIDTH
GR_END = XR_END + LRU_WIDTH
GA_END = GR_END + D_MODEL
IN_WIDTH = GA_END + D_MODEL

SUBLANES = 8
LANES = 128
MXU_DIM = 256
KEYS_PAD = 256
GATE_BLOCKS = LRU_WIDTH // MXU_DIM
ROUTER_LANES = LANES
VMEM_LIMIT_BYTES = 56 * 1024 * 1024

BF16 = jnp.bfloat16
F32 = jnp.float32


def _rmsnorm(x, g):
    ms = jnp.mean(x * x, axis=-1, keepdims=True)
    return (x * lax.rsqrt(ms + RMS_EPS)) * g


def _rope_slab(x, cos, sin_signed, first_half):
    partner = jnp.where(first_half, pltpu.roll(x, LANES - HALF_DIM, 1), pltpu.roll(x, HALF_DIM, 1))
    return x * cos + partner * sin_signed


def _dot(a, b):
    return jnp.dot(a, b, preferred_element_type=F32)


def _lru_scan(a, b, h0, acum, bcum, hbuf):
    t_rows = a.shape[0]
    row_in_group = lax.broadcasted_iota(jnp.int32, a.shape, 0) & (SUBLANES - 1)
    for s in (1, 2, 4):
        a_s = pltpu.roll(a, s, 0)
        b_s = pltpu.roll(b, s, 0)
        m = row_in_group >= s
        b = jnp.where(m, a * b_s + b, b)
        a = jnp.where(m, a * a_s, a)
    acum[...] = a
    bcum[...] = b
    h = h0
    for g in range(t_rows // SUBLANES):
        rows = pl.ds(g * SUBLANES, SUBLANES)
        hg = acum[rows, :] * h + bcum[rows, :]
        hbuf[rows, :] = hg
        h = hg[SUBLANES - 1:SUBLANES, :]
    return hbuf[...], h


def _mixer_kernel(sinks_ref, x_ref, cos_ref, sin_ref, kwin0_ref, vwin0_ref, conv0_ref, h0_ref,
                  kmeta_ref, vmeta_ref, gmix_ref, win_ref, wao_ref, wro_ref, wout_ref,
                  convw_ref, convb_ref, wgate_ref, br_ref, bi_ref, lam_ref,
                  hres_ref, knew_ref, vnew_ref, convnew_ref, lrunew_ref,
                  kbuf, vbuf, xrbuf, hcar, acum, bcum, hbuf, yabuf,
                  *, tile, chunk, hist0, meta_valid):
    s = pl.program_id(1)
    n_s = pl.num_programs(1)

    @pl.when(s == 0)
    def _init():
        kbuf[0:WINDOW, :] = kwin0_ref[0]
        vbuf[0:WINDOW, :] = vwin0_ref[0]
        xrbuf[0:SUBLANES, :] = conv0_ref[0]
        hcar[...] = h0_ref[0]

    x = x_ref[0]
    nb = _rmsnorm(x, gmix_ref[...]).astype(BF16)

    lane = lax.broadcasted_iota(jnp.int32, (1, LANES), 1)
    first_half = (lane & (HEAD_DIM - 1)) < HALF_DIM
    low_head = lane < HEAD_DIM
    cos = cos_ref[...]
    sin = sin_ref[...]

    q = _dot(nb, win_ref[:, 0:Q_END]) * (HEAD_DIM ** -0.5)
    kv = _dot(nb, win_ref[:, Q_END:V_END])
    k = _rope_slab(kv[:, 0:KV_WIDTH], cos, sin, first_half)
    v = kv[:, KV_WIDTH:2 * KV_WIDTH]
    kbuf[WINDOW:WINDOW + tile, :] = k
    vbuf[WINDOW:WINDOW + tile, :] = v
    qslabs = [
        _rope_slab(q[:, j * LANES:(j + 1) * LANES], cos, sin, first_half).astype(BF16)
        for j in range(ATTN_WIDTH // LANES)
    ]

    col = lax.broadcasted_iota(jnp.int32, (1, KEYS_PAD), 1)
    pairs_per_kv = Q_PER_KV // 2
    rows = pairs_per_kv * chunk
    slab_of_row = lax.broadcasted_iota(jnp.int32, (rows, 1), 0) // chunk
    for ci in range(tile // chunk):
        r0 = ci * chunk
        hist = jnp.minimum(WINDOW, hist0 + s * tile + r0)
        valid = ((col < WINDOW) & (col >= WINDOW - hist)) | ((col >= WINDOW) & (col < WINDOW + chunk))
        if meta_valid:
            valid = valid | ((col >= WINDOW + chunk) & (col < WINDOW + chunk + N_META))
        kcat = jnp.concatenate([kbuf[r0:r0 + WINDOW + chunk, :], kmeta_ref[...]], axis=0)
        vcat = jnp.concatenate([vbuf[r0:r0 + WINDOW + chunk, :], vmeta_ref[...]], axis=0)
        kswap = pltpu.roll(kcat, HEAD_DIM, 1)
        vswap = pltpu.roll(vcat, HEAD_DIM, 1)
        for g in range(N_KV_HEADS):
            k_lo, k_hi = (kcat, kswap) if g == 0 else (kswap, kcat)
            v_lo, v_hi = (vcat, vswap) if g == 0 else (vswap, vcat)
            kpair = jnp.concatenate([jnp.where(low_head, k_lo, 0.0), jnp.where(low_head, 0.0, k_hi)], axis=0)
            vpair = jnp.concatenate([jnp.where(low_head, v_lo, 0.0), jnp.where(low_head, 0.0, v_hi)], axis=0)
            kpair = kpair.astype(BF16)
            vpair = vpair.astype(BF16)
            qg = jnp.concatenate(
                [qslabs[g * pairs_per_kv + p][r0:r0 + chunk, :] for p in range(pairs_per_kv)], axis=0)
            sc = lax.dot_general(qg, kpair, (((1,), (1,)), ((), ())), preferred_element_type=F32)
            e_parts = []
            inv_parts = []
            for half in range(2):
                sink = jnp.zeros((rows, 1), F32)
                for p in range(pairs_per_kv):
                    sink = jnp.where(slab_of_row == p, sinks_ref[g * Q_PER_KV + 2 * p + half], sink)
                sh = jnp.where(valid, sc[:, half * KEYS_PAD:(half + 1) * KEYS_PAD], NEG_INF)
                m = jnp.maximum(jnp.max(sh, axis=-1, keepdims=True), sink)
                e = jnp.exp(sh - m)
                denom = jnp.sum(e, axis=-1, keepdims=True) + jnp.exp(sink - m)
                e_parts.append(e.astype(BF16))
                inv_parts.append(1.0 / denom)
            o = _dot(jnp.concatenate(e_parts, axis=1), vpair)
            o = o * jnp.where(low_head, inv_parts[0], inv_parts[1])
            for p in range(pairs_per_kv):
                c0 = (g * pairs_per_kv + p) * LANES
                yabuf[r0:r0 + chunk, c0:c0 + LANES] = o[p * chunk:(p + 1) * chunk, :].astype(BF16)

    xr = _dot(nb, win_ref[:, V_END:XR_END])
    xrbuf[SUBLANES:SUBLANES + tile, :] = xr
    u = convb_ref[...]
    for tap in range(CONV_WIDTH):
        start = SUBLANES - (CONV_WIDTH - 1) + tap
        u = u + xrbuf[start:start + tile, :] * convw_ref[tap:tap + 1, :]
    ub = u.astype(BF16)
    r_parts = []
    i_parts = []
    for blk in range(GATE_BLOCKS):
        ri = _dot(ub[:, blk * MXU_DIM:(blk + 1) * MXU_DIM], wgate_ref[blk])
        r_parts.append(ri[:, 0:MXU_DIM])
        i_parts.append(ri[:, MXU_DIM:2 * MXU_DIM])
    r = jax.nn.sigmoid(jnp.concatenate(r_parts, axis=1) + br_ref[...])
    i = jax.nn.sigmoid(jnp.concatenate(i_parts, axis=1) + bi_ref[...])
    neg_lam = -lam_ref[...]
    softplus = jnp.maximum(neg_lam, 0.0) + jnp.log1p(jnp.exp(-jnp.abs(neg_lam)))
    log_a = (-LRU_C * r) * softplus
    a = jnp.exp(log_a)
    th = jnp.tanh(log_a)
    b = jnp.sqrt((-2.0 * th) / (1.0 - th)) * (i * u)
    h, h_last = _lru_scan(a, b, hcar[...], acum, bcum, hbuf)

    gr = _dot(nb, win_ref[:, XR_END:GR_END])
    yb = (h * jax.nn.gelu(gr)).astype(BF16)

    t_attn = _dot(yabuf[...], wao_ref[...])
    t_rec = _dot(yb, wro_ref[...])
    ga = _dot(nb, win_ref[:, GR_END:GA_END])
    gb = _dot(nb, win_ref[:, GA_END:IN_WIDTH])
    mix = (jax.nn.sigmoid(ga) * t_attn + jax.nn.sigmoid(gb) * t_rec).astype(BF16)
    hres_ref[0] = x + _dot(mix, wout_ref[...])

    knext = kbuf[tile:tile + WINDOW, :]
    vnext = vbuf[tile:tile + WINDOW, :]
    xrnext = xrbuf[tile:tile + SUBLANES, :]
    kbuf[0:WINDOW, :] = knext
    vbuf[0:WINDOW, :] = vnext
    xrbuf[0:SUBLANES, :] = xrnext
    hcar[...] = h_last

    @pl.when(s == n_s - 1)
    def _emit_state():
        knew_ref[0] = knext
        vnew_ref[0] = vnext
        convnew_ref[0] = xrnext
        lrunew_ref[0] = h_last


def _const_spec(shape):
    nd = len(shape)
    return pl.BlockSpec(shape, lambda b, s, _nd=nd: (0,) * _nd, pipeline_mode=pl.Buffered(1))


def _mixer(x, cos, sin, kwin0, vwin0, conv0, h0, kmeta, vmeta, wts, *, tile, chunk, hist0, meta_valid):
    n_b, seq, _ = x.shape
    assert seq % tile == 0 and tile % chunk == 0 and tile % SUBLANES == 0
    assert kmeta.shape == (KEYS_PAD - WINDOW - chunk, KV_WIDTH)
    per_batch_state = kwin0.shape[0] == n_b

    def state_map(b, s):
        return (b if per_batch_state else 0, 0, 0)

    in_specs = [
        pl.BlockSpec(memory_space=pltpu.SMEM),
        pl.BlockSpec((1, tile, D_MODEL), lambda b, s: (b, s, 0)),
        pl.BlockSpec((tile, LANES), lambda b, s: (s, 0)),
        pl.BlockSpec((tile, LANES), lambda b, s: (s, 0)),
        pl.BlockSpec((1, WINDOW, KV_WIDTH), state_map),
        pl.BlockSpec((1, WINDOW, KV_WIDTH), state_map),
        pl.BlockSpec((1, SUBLANES, LRU_WIDTH), state_map),
        pl.BlockSpec((1, 1, LRU_WIDTH), state_map),
        _const_spec(kmeta.shape), _const_spec(vmeta.shape),
        _const_spec((1, D_MODEL)),
        _const_spec((D_MODEL, IN_WIDTH)),
        _const_spec((ATTN_WIDTH, D_MODEL)), _const_spec((LRU_WIDTH, D_MODEL)), _const_spec((D_MODEL, D_MODEL)),
        _const_spec((CONV_WIDTH, LRU_WIDTH)), _const_spec((1, LRU_WIDTH)),
        _const_spec((GATE_BLOCKS, MXU_DIM, 2 * MXU_DIM)),
        _const_spec((1, LRU_WIDTH)), _const_spec((1, LRU_WIDTH)), _const_spec((1, LRU_WIDTH)),
    ]
    out_shape = (
        jax.ShapeDtypeStruct((n_b, seq, D_MODEL), F32),
        jax.ShapeDtypeStruct((n_b, WINDOW, KV_WIDTH), F32),
        jax.ShapeDtypeStruct((n_b, WINDOW, KV_WIDTH), F32),
        jax.ShapeDtypeStruct((n_b, SUBLANES, LRU_WIDTH), F32),
        jax.ShapeDtypeStruct((n_b, 1, LRU_WIDTH), F32),
    )
    out_specs = (
        pl.BlockSpec((1, tile, D_MODEL), lambda b, s: (b, s, 0)),
        pl.BlockSpec((1, WINDOW, KV_WIDTH), lambda b, s: (b, 0, 0)),
        pl.BlockSpec((1, WINDOW, KV_WIDTH), lambda b, s: (b, 0, 0)),
        pl.BlockSpec((1, SUBLANES, LRU_WIDTH), lambda b, s: (b, 0, 0)),
        pl.BlockSpec((1, 1, LRU_WIDTH), lambda b, s: (b, 0, 0)),
    )
    scratch = [
        pltpu.VMEM((WINDOW + tile, KV_WIDTH), F32),
        pltpu.VMEM((WINDOW + tile, KV_WIDTH), F32),
        pltpu.VMEM((SUBLANES + tile, LRU_WIDTH), F32),
        pltpu.VMEM((1, LRU_WIDTH), F32),
        pltpu.VMEM((tile, LRU_WIDTH), F32),
        pltpu.VMEM((tile, LRU_WIDTH), F32),
        pltpu.VMEM((tile, LRU_WIDTH), F32),
        pltpu.VMEM((tile, ATTN_WIDTH), BF16),
    ]
    kern = functools.partial(_mixer_kernel, tile=tile, chunk=chunk, hist0=hist0, meta_valid=meta_valid)
    return pl.pallas_call(
        kern,
        grid=(n_b, seq // tile),
        in_specs=in_specs,
        out_specs=out_specs,
        out_shape=out_shape,
        scratch_shapes=scratch,
        compiler_params=pltpu.CompilerParams(
            dimension_semantics=("arbitrary", "arbitrary"), vmem_limit_bytes=VMEM_LIMIT_BYTES),
        name=f"mixer_t{tile}_c{chunk}",
    )(wts["sinks"], x, cos, sin, kwin0, vwin0, conv0, h0, kmeta, vmeta, wts["g_mix"], wts["w_in"],
      wts["w_attn_o"], wts["w_rec_o"], wts["w_out"], wts["conv_w"], wts["conv_b"], wts["w_gate"],
      wts["b_r"], wts["b_i"], wts["lam"])


def _channel_kernel(h_ref, gffn_ref, wr_ref, wg_ref, wu_ref, wd_ref, gfin_ref, y_ref, acc):
    h = h_ref[...]
    nb = _rmsnorm(h, gffn_ref[...]).astype(BF16)
    logits = _dot(nb, wr_ref[...])
    lane = lax.broadcasted_iota(jnp.int32, logits.shape, 1)
    neg = -jnp.inf
    big = jnp.int32(ROUTER_LANES)

    gmask = (lane >= N_EXPERTS) & (lane < N_EXPERTS + N_GROUPS)
    gl = jnp.where(gmask, logits, neg)
    gmax = jnp.max(gl, axis=-1, keepdims=True)
    gidx = jnp.min(jnp.where(gl == gmax, lane - N_EXPERTS, big), axis=-1, keepdims=True)
    g_w = 1.0 / jnp.sum(jnp.where(gmask, jnp.exp(gl - gmax), 0.0), axis=-1, keepdims=True)

    emask = (lane < N_EXPERTS) & ((lane // EXPERTS_PER_GROUP) == gidx)
    el = jnp.where(emask, logits, neg)
    v1 = jnp.max(el, axis=-1, keepdims=True)
    i1 = jnp.min(jnp.where(el == v1, lane, big), axis=-1, keepdims=True)
    el2 = jnp.where(lane == i1, neg, el)
    v2 = jnp.max(el2, axis=-1, keepdims=True)
    i2 = jnp.min(jnp.where(el2 == v2, lane, big), axis=-1, keepdims=True)
    e2 = jnp.exp(v2 - v1)
    den = 1.0 + e2
    w1 = (1.0 / den) * g_w
    w2 = (e2 / den) * g_w
    comb = jnp.where(lane == i1, w1, 0.0) + jnp.where(lane == i2, w2, 0.0)

    acc[...] = jnp.zeros_like(acc)
    for e in range(N_EXPERTS):
        hg = _dot(nb, wg_ref[e])
        hu = _dot(nb, wu_ref[e])
        act = (jax.nn.silu(hg) * hu) * comb[:, e:e + 1]
        acc[...] += _dot(act.astype(BF16), wd_ref[e])
    y_ref[...] = _rmsnorm(h + acc[...], gfin_ref[...])


def _channel(h, wts, *, tile):
    n_tok = h.shape[0]
    assert n_tok % tile == 0

    def const(shape):
        nd = len(shape)
        return pl.BlockSpec(shape, lambda i, _nd=nd: (0,) * _nd, pipeline_mode=pl.Buffered(1))

    return pl.pallas_call(
        _channel_kernel,
        grid=(n_tok // tile,),
        in_specs=[
            pl.BlockSpec((tile, D_MODEL), lambda i: (i, 0)),
            const((1, D_MODEL)),
            const((D_MODEL, ROUTER_LANES)),
            const((N_EXPERTS, D_MODEL, D_EXPERT)),
            const((N_EXPERTS, D_MODEL, D_EXPERT)),
            const((N_EXPERTS, D_EXPERT, D_MODEL)),
            const((1, D_MODEL)),
        ],
        out_specs=pl.BlockSpec((tile, D_MODEL), lambda i: (i, 0)),
        out_shape=jax.ShapeDtypeStruct((n_tok, D_MODEL), F32),
        scratch_shapes=[pltpu.VMEM((tile, D_MODEL), F32)],
        compiler_params=pltpu.CompilerParams(
            dimension_semantics=("arbitrary",), vmem_limit_bytes=VMEM_LIMIT_BYTES),
        name=f"channel_t{tile}",
    )(h, wts["g_ffn"], wts["w_router"], wts["w_gate_e"], wts["w_up_e"], wts["w_down_e"], wts["g_final"])


def _rope_tables(pos):
    inv_freq = ROPE_THETA ** (-jnp.arange(0, HEAD_DIM, 2, dtype=F32) / HEAD_DIM)
    ang = pos.astype(F32)[:, None] * inv_freq[None, :]
    cos = jnp.cos(ang)
    sin = jnp.sin(ang)
    reps = LANES // HEAD_DIM
    return jnp.tile(jnp.concatenate([cos, cos], axis=1), (1, reps)), jnp.tile(jnp.concatenate([-sin, sin], axis=1), (1, reps))


def _block_diag_gates(w_r, w_i):
    per = MXU_DIM // LRU_BLOCK_DIM

    def bd(w):
        w = w.reshape(GATE_BLOCKS, per, LRU_BLOCK_DIM, LRU_BLOCK_DIM)
        eye = jnp.eye(per, dtype=w.dtype)
        full = jnp.einsum("bpde,pq->bpdqe", w, eye)
        return full.reshape(GATE_BLOCKS, MXU_DIM, MXU_DIM)

    return jnp.concatenate([bd(w_r), bd(w_i)], axis=-1).astype(BF16)


def _pad_meta(t, chunk):
    return jnp.pad(t, ((0, KEYS_PAD - WINDOW - chunk - N_META), (0, 0)))


def kernel(x_prompt, x_sample, cache_attn_k, cache_attn_v, state_conv, state_lru, meta_tokens,
           norm_mix_g, w_in, attn_sinks, w_attn_o, conv_w, conv_b, w_rg_r, b_rg_r, w_rg_i, b_rg_i,
           lru_lambda, w_rec_o, w_out, norm_ffn_g, w_group_router, w_expert_router, w_gate_e, w_up_e,
           w_down_e, norm_final_g):
    n_b, seq, _ = x_prompt.shape
    n_db, dseq, _ = x_sample.shape
    past_len = 2048
    layer = 0
    row = lambda t: t.reshape(1, -1)
    w_router = jnp.concatenate([w_expert_router[layer], w_group_router[layer]], axis=1)
    w_router = jnp.pad(w_router, ((0, 0), (0, ROUTER_LANES - N_EXPERTS - N_GROUPS)))
    wts = dict(
        sinks=attn_sinks[layer], g_mix=row(norm_mix_g[layer]), w_in=w_in[layer].astype(BF16),
        w_attn_o=w_attn_o[layer].astype(BF16), w_rec_o=w_rec_o[layer].astype(BF16), w_out=w_out[layer].astype(BF16),
        conv_w=conv_w[layer], conv_b=row(conv_b[layer]), w_gate=_block_diag_gates(w_rg_r[layer], w_rg_i[layer]),
        b_r=row(b_rg_r[layer]), b_i=row(b_rg_i[layer]), lam=row(lru_lambda[layer]),
        g_ffn=row(norm_ffn_g[layer]), w_router=w_router.astype(BF16), w_gate_e=w_gate_e[layer].astype(BF16),
        w_up_e=w_up_e[layer].astype(BF16), w_down_e=w_down_e[layer].astype(BF16), g_final=row(norm_final_g),
    )

    zeros_win = jnp.zeros((1, WINDOW, KV_WIDTH), F32)
    zeros_conv = jnp.zeros((1, SUBLANES, LRU_WIDTH), F32)
    zeros_h = jnp.zeros((1, 1, LRU_WIDTH), F32)

    cos_m, sin_m = _rope_tables(jnp.arange(N_META, dtype=jnp.int32))
    zero_meta = jnp.zeros((KEYS_PAD - WINDOW - N_META, KV_WIDTH), F32)
    _, k_m, v_m, conv_m, h_m = _mixer(
        meta_tokens[None], cos_m, sin_m, zeros_win, zeros_win, zeros_conv, zeros_h, zero_meta, zero_meta, wts,
        tile=N_META, chunk=N_META, hist0=0, meta_valid=False)
    k_meta = k_m[0, WINDOW - N_META:]
    v_meta = v_m[0, WINDOW - N_META:]

    tile_p = 256 if seq % 256 == 0 else CHUNK
    cos_p, sin_p = _rope_tables(N_META + jnp.arange(seq, dtype=jnp.int32))
    h_p, k_p, v_p, conv_p, lru_p = _mixer(
        x_prompt, cos_p, sin_p, zeros_win, zeros_win, conv_m, h_m, _pad_meta(k_meta, CHUNK), _pad_meta(v_meta, CHUNK),
        wts, tile=tile_p, chunk=CHUNK, hist0=0, meta_valid=True)

    cos_s, sin_s = _rope_tables(N_META + past_len + jnp.arange(dseq, dtype=jnp.int32))
    conv_s0 = jnp.pad(state_conv[layer], ((0, 0), (SUBLANES - (CONV_WIDTH - 1), 0), (0, 0)))
    h_s, k_s, v_s, conv_s, lru_s = _mixer(
        x_sample, cos_s, sin_s, cache_attn_k[layer].reshape(n_db, WINDOW, KV_WIDTH),
        cache_attn_v[layer].reshape(n_db, WINDOW, KV_WIDTH), conv_s0, state_lru[layer][:, None, :],
        _pad_meta(k_meta, dseq), _pad_meta(v_meta, dseq), wts, tile=dseq, chunk=dseq, hist0=WINDOW, meta_valid=True)

    tok_p = n_b * seq
    tok_s = n_db * dseq
    y_p = _channel(h_p.reshape(tok_p, D_MODEL), wts, tile=512 if tok_p % 512 == 0 else CHUNK)
    y_s = _channel(h_s.reshape(tok_s, D_MODEL), wts, tile=tok_s)

    keep = SUBLANES - (CONV_WIDTH - 1)
    kv_shape = lambda t: t.reshape(1, t.shape[0], WINDOW, N_KV_HEADS, HEAD_DIM)
    return (y_p.reshape(n_b, seq, D_MODEL), y_s.reshape(n_db, dseq, D_MODEL),
            kv_shape(k_p), kv_shape(v_p), conv_p[None, :, keep:, :], lru_p.reshape(1, n_b, LRU_WIDTH),
            kv_shape(k_s), kv_shape(v_s), conv_s[None, :, keep:, :], lru_s.reshape(1, n_db, LRU_WIDTH))
```

```python
import functools
import math

import jax
import jax.numpy as jnp
from jax import lax
from jax.experimental import pallas as pl
from jax.experimental.pallas import tpu as pltpu

D_MODEL = 1024
CHUNK = 64
N_META = 16
N_HEADS = 16
N_KV_HEADS = 2
HEAD_DIM = 64
HALF_DIM = HEAD_DIM // 2
Q_PER_KV = N_HEADS // N_KV_HEADS
ATTN_WIDTH = N_HEADS * HEAD_DIM
KV_WIDTH = N_KV_HEADS * HEAD_DIM
WINDOW = 128
ROPE_THETA = 10000.0
LRU_WIDTH = D_MODEL
LRU_BLOCKS = 16
LRU_BLOCK_DIM = LRU_WIDTH // LRU_BLOCKS
LRU_C = 8.0
CONV_WIDTH = 4
N_GROUPS = 4
EXPERTS_PER_GROUP = 4
N_EXPERTS = N_GROUPS * EXPERTS_PER_GROUP
D_EXPERT = 256
RMS_EPS = 1e-6
NEG_INF = -1e30
Q_END = ATTN_WIDTH
K_END = Q_END + KV_WIDTH
V_END = K_END + KV_WIDTH
XR_END = V_END + LRU_WIDTH
GR_END = XR_END + LRU_WIDTH
GA_END = GR_END + D_MODEL
IN_WIDTH = GA_END + D_MODEL
LOG2E = math.log2(math.e)

SUBLANES = 8
LANES = 128
MXU_DIM = 256
KEYS_PAD = 256
GATE_BLOCKS = LRU_WIDTH // MXU_DIM
ROUTER_LANES = LANES
EXPERT_ROWS = 128
GROUP_WIDTH = EXPERTS_PER_GROUP * D_EXPERT
VMEM_LIMIT_BYTES = 56 * 1024 * 1024

BF16 = jnp.bfloat16
F32 = jnp.float32


def _rmsnorm(x, g):
    ms = jnp.mean(x * x, axis=-1, keepdims=True)
    return (x * lax.rsqrt(ms + RMS_EPS)) * g


def _sigmoid(x):
    return 0.5 * jnp.tanh(0.5 * x) + 0.5


def _rope_slab(x, cos, sin_signed, first_half):
    partner = jnp.where(first_half, pltpu.roll(x, LANES - HALF_DIM, 1), pltpu.roll(x, HALF_DIM, 1))
    return x * cos + partner * sin_signed


def _dot(a, b):
    return jnp.dot(a, b, preferred_element_type=F32)


def _lru_scan(a, b, h0):
    t_rows, width = a.shape
    row = lax.broadcasted_iota(jnp.int32, (SUBLANES, width), 0)
    h = h0
    hs = []
    for g in range(t_rows // SUBLANES):
        ag = a[g * SUBLANES:(g + 1) * SUBLANES, :]
        bg = b[g * SUBLANES:(g + 1) * SUBLANES, :]
        for s in (1, 2, 4):
            keep = row >= s
            a_s = jnp.where(keep, pltpu.roll(ag, s, 0), 1.0)
            b_s = jnp.where(keep, pltpu.roll(bg, s, 0), 0.0)
            bg = ag * b_s + bg
            ag = ag * a_s
        hg = ag * h + bg
        hs.append(hg)
        h = hg[SUBLANES - 1:SUBLANES, :]
    return jnp.concatenate(hs, axis=0), h


def _mixer_kernel(sinks_ref, x_ref, cos_ref, sin_ref, kwin0_ref, vwin0_ref, conv0_ref, h0_ref,
                  kmeta_ref, vmeta_ref, gmix_ref, win_ref, wao_ref, wro_ref, wout_ref,
                  convw_ref, convb_ref, wgate_ref, br_ref, bi_ref, lam_ref,
                  hres_ref, knew_ref, vnew_ref, convnew_ref, lrunew_ref,
                  kbuf, vbuf, xrbuf, hcar, mbuf,
                  *, tile, chunk, hist0, meta_valid):
    s = pl.program_id(1)
    n_s = pl.num_programs(1)

    @pl.when(s == 0)
    def _init():
        kbuf[0:WINDOW, :] = kwin0_ref[0]
        vbuf[0:WINDOW, :] = vwin0_ref[0]
        xrbuf[0:SUBLANES, :] = conv0_ref[0]
        hcar[...] = h0_ref[0]

    x = x_ref[0]
    nb = _rmsnorm(x, gmix_ref[...]).astype(BF16)

    lane = lax.broadcasted_iota(jnp.int32, (1, LANES), 1)
    first_half = (lane & (HEAD_DIM - 1)) < HALF_DIM
    low_head = lane < HEAD_DIM
    cos = cos_ref[...]
    sin = sin_ref[...]

    q = _dot(nb, win_ref[:, 0:Q_END]) * (LOG2E * HEAD_DIM ** -0.5)
    kv = _dot(nb, win_ref[:, Q_END:V_END])
    xr = _dot(nb, win_ref[:, V_END:XR_END])
    kbuf[WINDOW:WINDOW + tile, :] = _rope_slab(kv[:, 0:KV_WIDTH], cos, sin, first_half)
    vbuf[WINDOW:WINDOW + tile, :] = kv[:, KV_WIDTH:2 * KV_WIDTH]
    xrbuf[SUBLANES:SUBLANES + tile, :] = xr
    qslabs = [
        _rope_slab(q[:, j * LANES:(j + 1) * LANES], cos, sin, first_half).astype(BF16)
        for j in range(ATTN_WIDTH // LANES)
    ]

    xr_ext = xrbuf[...]
    z = xr_ext * convw_ref[0:1, :]
    for tap in range(1, CONV_WIDTH):
        z = pltpu.roll(z, 1, 0) + xr_ext * convw_ref[tap:tap + 1, :]
    u = z[SUBLANES:, :] + convb_ref[...]
    ub = u.astype(BF16)
    r_parts = []
    i_parts = []
    for blk in range(GATE_BLOCKS):
        ri = _dot(ub[:, blk * MXU_DIM:(blk + 1) * MXU_DIM], wgate_ref[blk])
        r_parts.append(ri[:, 0:MXU_DIM])
        i_parts.append(ri[:, MXU_DIM:2 * MXU_DIM])

    col = lax.broadcasted_iota(jnp.int32, (1, KEYS_PAD), 1)
    pairs_per_kv = Q_PER_KV // 2
    rows = pairs_per_kv * chunk
    slab_of_row = lax.broadcasted_iota(jnp.int32, (rows, 1), 0) // chunk
    n_chunks = tile // chunk
    scores = []
    vpairs = []
    valids = []
    for ci in range(n_chunks):
        r0 = ci * chunk
        hist = jnp.minimum(WINDOW, hist0 + s * tile + r0)
        valid = ((col < WINDOW) & (col >= WINDOW - hist)) | ((col >= WINDOW) & (col < WINDOW + chunk))
        if meta_valid:
            valid = valid | ((col >= WINDOW + chunk) & (col < WINDOW + chunk + N_META))
        valids.append(valid)
        kcat = jnp.concatenate([kbuf[r0:r0 + WINDOW + chunk, :], kmeta_ref[...]], axis=0)
        vcat = jnp.concatenate([vbuf[r0:r0 + WINDOW + chunk, :], vmeta_ref[...]], axis=0)
        kswap = pltpu.roll(kcat, HEAD_DIM, 1)
        vswap = pltpu.roll(vcat, HEAD_DIM, 1)
        for g in range(N_KV_HEADS):
            k_lo, k_hi = (kcat, kswap) if g == 0 else (kswap, kcat)
            v_lo, v_hi = (vcat, vswap) if g == 0 else (vswap, vcat)
            kpair = jnp.concatenate([jnp.where(low_head, k_lo, 0.0), jnp.where(low_head, 0.0, k_hi)], axis=0)
            vpair = jnp.concatenate([jnp.where(low_head, v_lo, 0.0), jnp.where(low_head, 0.0, v_hi)], axis=0)
            vpairs.append(vpair.astype(BF16))
            qg = jnp.concatenate(
                [qslabs[g * pairs_per_kv + p][r0:r0 + chunk, :] for p in range(pairs_per_kv)], axis=0)
            scores.append(lax.dot_general(qg, kpair.astype(BF16), (((1,), (1,)), ((), ())),
                                          preferred_element_type=F32))

    mbuf[0] = _dot(nb, win_ref[:, XR_END:GR_END])
    mbuf[1] = _dot(nb, win_ref[:, GR_END:GA_END])
    mbuf[2] = _dot(nb, win_ref[:, GA_END:IN_WIDTH])

    r = _sigmoid(jnp.concatenate(r_parts, axis=1) + br_ref[...])
    i = _sigmoid(jnp.concatenate(i_parts, axis=1) + bi_ref[...])
    neg_lam = -lam_ref[...]
    softplus = jnp.maximum(neg_lam, 0.0) + jnp.log1p(jnp.exp(-jnp.abs(neg_lam)))
    log_a = (-LRU_C * r) * softplus
    a = jnp.exp(log_a)
    th = jnp.tanh(log_a)
    b = jnp.sqrt((-2.0 * th) / (1.0 - th)) * (i * u)
    h, h_last = _lru_scan(a, b, hcar[...])

    ya_chunks = []
    for ci in range(n_chunks):
        slabs = [None] * (N_KV_HEADS * pairs_per_kv)
        for g in range(N_KV_HEADS):
            sc = scores[ci * N_KV_HEADS + g]
            e_parts = []
            inv_parts = []
            for half in range(2):
                sink = jnp.zeros((rows, 1), F32)
                for p in range(pairs_per_kv):
                    sink = jnp.where(slab_of_row == p, sinks_ref[g * Q_PER_KV + 2 * p + half] * LOG2E, sink)
                sh = jnp.where(valids[ci], sc[:, half * KEYS_PAD:(half + 1) * KEYS_PAD], NEG_INF)
                m = jnp.maximum(jnp.max(sh, axis=-1, keepdims=True), sink)
                e = jnp.exp2(sh - m)
                denom = jnp.sum(e, axis=-1, keepdims=True) + jnp.exp2(sink - m)
                e_parts.append(e.astype(BF16))
                inv_parts.append(1.0 / denom)
            o = _dot(jnp.concatenate(e_parts, axis=1), vpairs[ci * N_KV_HEADS + g])
            o = o * jnp.where(low_head, inv_parts[0], inv_parts[1])
            for p in range(pairs_per_kv):
                slabs[g * pairs_per_kv + p] = o[p * chunk:(p + 1) * chunk, :]
        ya_chunks.append(jnp.concatenate(slabs, axis=1))
    ya = jnp.concatenate(ya_chunks, axis=0).astype(BF16)

    t_attn = _dot(ya, wao_ref[...])
    yb = (h * jax.nn.gelu(mbuf[0])).astype(BF16)
    t_rec = _dot(yb, wro_ref[...])
    mix = (_sigmoid(mbuf[1]) * t_attn + _sigmoid(mbuf[2]) * t_rec).astype(BF16)
    hres_ref[0] = x + _dot(mix, wout_ref[...])

    knext = kbuf[tile:tile + WINDOW, :]
    vnext = vbuf[tile:tile + WINDOW, :]
    xrnext = xrbuf[tile:tile + SUBLANES, :]
    kbuf[0:WINDOW, :] = knext
    vbuf[0:WINDOW, :] = vnext
    xrbuf[0:SUBLANES, :] = xrnext
    hcar[...] = h_last

    @pl.when(s == n_s - 1)
    def _emit_state():
        knew_ref[0] = knext
        vnew_ref[0] = vnext
        convnew_ref[0] = xrnext
        lrunew_ref[0] = h_last


def _const_spec(shape):
    nd = len(shape)
    return pl.BlockSpec(shape, lambda b, s, _nd=nd: (0,) * _nd, pipeline_mode=pl.Buffered(1))


def _mixer(x, cos, sin, kwin0, vwin0, conv0, h0, kmeta, vmeta, wts, *, tile, chunk, hist0, meta_valid):
    n_b, seq, _ = x.shape
    assert seq % tile == 0 and tile % chunk == 0 and tile % SUBLANES == 0
    assert kmeta.shape == (KEYS_PAD - WINDOW - chunk, KV_WIDTH)
    per_batch_state = kwin0.shape[0] == n_b

    def state_map(b, s):
        return (b if per_batch_state else 0, 0, 0)

    in_specs = [
        pl.BlockSpec(memory_space=pltpu.SMEM),
        pl.BlockSpec((1, tile, D_MODEL), lambda b, s: (b, s, 0)),
        pl.BlockSpec((tile, LANES), lambda b, s: (s, 0)),
        pl.BlockSpec((tile, LANES), lambda b, s: (s, 0)),
        pl.BlockSpec((1, WINDOW, KV_WIDTH), state_map),
        pl.BlockSpec((1, WINDOW, KV_WIDTH), state_map),
        pl.BlockSpec((1, SUBLANES, LRU_WIDTH), state_map),
        pl.BlockSpec((1, 1, LRU_WIDTH), state_map),
        _const_spec(kmeta.shape), _const_spec(vmeta.shape),
        _const_spec((1, D_MODEL)),
        _const_spec((D_MODEL, IN_WIDTH)),
        _const_spec((ATTN_WIDTH, D_MODEL)), _const_spec((LRU_WIDTH, D_MODEL)), _const_spec((D_MODEL, D_MODEL)),
        _const_spec((CONV_WIDTH, LRU_WIDTH)), _const_spec((1, LRU_WIDTH)),
        _const_spec((GATE_BLOCKS, MXU_DIM, 2 * MXU_DIM)),
        _const_spec((1, LRU_WIDTH)), _const_spec((1, LRU_WIDTH)), _const_spec((1, LRU_WIDTH)),
    ]
    out_shape = (
        jax.ShapeDtypeStruct((n_b, seq, D_MODEL), F32),
        jax.ShapeDtypeStruct((n_b, WINDOW, KV_WIDTH), F32),
        jax.ShapeDtypeStruct((n_b, WINDOW, KV_WIDTH), F32),
        jax.ShapeDtypeStruct((n_b, SUBLANES, LRU_WIDTH), F32),
        jax.ShapeDtypeStruct((n_b, 1, LRU_WIDTH), F32),
    )
    out_specs = (
        pl.BlockSpec((1, tile, D_MODEL), lambda b, s: (b, s, 0)),
        pl.BlockSpec((1, WINDOW, KV_WIDTH), lambda b, s: (b, 0, 0)),
        pl.BlockSpec((1, WINDOW, KV_WIDTH), lambda b, s: (b, 0, 0)),
        pl.BlockSpec((1, SUBLANES, LRU_WIDTH), lambda b, s: (b, 0, 0)),
        pl.BlockSpec((1, 1, LRU_WIDTH), lambda b, s: (b, 0, 0)),
    )
    scratch = [
        pltpu.VMEM((WINDOW + tile, KV_WIDTH), F32),
        pltpu.VMEM((WINDOW + tile, KV_WIDTH), F32),
        pltpu.VMEM((SUBLANES + tile, LRU_WIDTH), F32),
        pltpu.VMEM((1, LRU_WIDTH), F32),
        pltpu.VMEM((3, tile, D_MODEL), F32),
    ]
    kern = functools.partial(_mixer_kernel, tile=tile, chunk=chunk, hist0=hist0, meta_valid=meta_valid)
    return pl.pallas_call(
        kern,
        grid=(n_b, seq // tile),
        in_specs=in_specs,
        out_specs=out_specs,
        out_shape=out_shape,
        scratch_shapes=scratch,
        compiler_params=pltpu.CompilerParams(
            dimension_semantics=("arbitrary", "arbitrary"), vmem_limit_bytes=VMEM_LIMIT_BYTES),
        name=f"mixer_t{tile}_c{chunk}",
    )(wts["sinks"], x, cos, sin, kwin0, vwin0, conv0, h0, kmeta, vmeta, wts["g_mix"], wts["w_in"],
      wts["w_attn_o"], wts["w_rec_o"], wts["w_out"], wts["conv_w"], wts["conv_b"], wts["w_gate"],
      wts["b_r"], wts["b_i"], wts["lam"])


def _split_bf16(v, parts):
    out = []
    for _ in range(parts):
        p = v.astype(BF16)
        out.append(p)
        v = v - p.astype(F32)
    return out


def _channel_kernel(h_ref, gffn_ref, wr_ref, wg_ref, wu_ref, wd_ref, gfin_ref, y_ref, xs, cs, acc, *, tile, rows):
    h = h_ref[...]
    nb = _rmsnorm(h, gffn_ref[...]).astype(BF16)
    logits = _dot(nb, wr_ref[...])
    lane = lax.broadcasted_iota(jnp.int32, logits.shape, 1)
    neg = -jnp.inf
    big = jnp.int32(ROUTER_LANES)

    gmask = (lane >= N_EXPERTS) & (lane < N_EXPERTS + N_GROUPS)
    gl = jnp.where(gmask, logits, neg)
    gmax = jnp.max(gl, axis=-1, keepdims=True)
    gidx = jnp.min(jnp.where(gl == gmax, lane - N_EXPERTS, big), axis=-1, keepdims=True)
    g_w = 1.0 / jnp.sum(jnp.where(gmask, jnp.exp(gl - gmax), 0.0), axis=-1, keepdims=True)

    emask = (lane < N_EXPERTS) & ((lane // EXPERTS_PER_GROUP) == gidx)
    el = jnp.where(emask, logits, neg)
    v1 = jnp.max(el, axis=-1, keepdims=True)
    i1 = jnp.min(jnp.where(el == v1, lane, big), axis=-1, keepdims=True)
    el2 = jnp.where(lane == i1, neg, el)
    v2 = jnp.max(el2, axis=-1, keepdims=True)
    i2 = jnp.min(jnp.where(el2 == v2, lane, big), axis=-1, keepdims=True)
    e2 = jnp.exp(v2 - v1)
    den = 1.0 + e2
    w1 = (1.0 / den) * g_w
    w2 = (e2 / den) * g_w
    comb = jnp.where(lane == i1, w1, 0.0) + jnp.where(lane == i2, w2, 0.0)

    g_col = gidx.astype(F32)
    g_row = jnp.transpose(jnp.broadcast_to(g_col, (tile, LANES)))[0:1, :]
    t_col = lax.broadcasted_iota(jnp.int32, (tile, 1), 0)
    t_row = lax.broadcasted_iota(jnp.int32, (1, tile), 1)
    a_before_b = (g_col < g_row) | ((g_col == g_row) & (t_col < t_row))
    pos_row = jnp.sum(a_before_b.astype(F32), axis=0, keepdims=True)
    pos_col = jnp.transpose(jnp.broadcast_to(pos_row, (LANES, tile)))[:, 0:1]
    perm = (pos_row == t_col.astype(F32)).astype(BF16)
    perm_t = (pos_col == t_row.astype(F32)).astype(BF16)

    xs[...] = _dot(perm, nb).astype(BF16)
    comb_parts = _split_bf16(comb, 3)
    cs[...] = (_dot(perm, comb_parts[0]) + _dot(perm, comb_parts[1])) + _dot(perm, comb_parts[2])

    start = [0.0] + [jnp.sum((g_col < float(g)).astype(F32)) for g in range(1, N_GROUPS)] + [float(tile)]

    acc[...] = jnp.zeros_like(acc)
    for c in range(tile // rows):
        blk = pl.ds(c * rows, rows)
        for g in range(N_GROUPS):
            @pl.when((start[g] < float((c + 1) * rows)) & (start[g + 1] > float(c * rows)))
            def _experts(blk=blk, g=g):
                xc = xs[blk, :]
                cc = cs[blk, :]
                hg = _dot(xc, wg_ref[g])
                hu = _dot(xc, wu_ref[g])
                weight = jnp.concatenate(
                    [jnp.broadcast_to(cc[:, e:e + 1], (rows, D_EXPERT))
                     for e in range(g * EXPERTS_PER_GROUP, (g + 1) * EXPERTS_PER_GROUP)], axis=1)
                act = ((hg * _sigmoid(hg)) * hu) * weight
                acc[blk, :] += _dot(act.astype(BF16), wd_ref[g])

    moe_parts = _split_bf16(acc[...], 2)
    moe = _dot(perm_t, moe_parts[0]) + _dot(perm_t, moe_parts[1])
    y_ref[...] = _rmsnorm(h + moe, gfin_ref[...])


def _channel(h, wts, *, tile):
    n_tok = h.shape[0]
    rows = min(tile, EXPERT_ROWS)
    assert n_tok % tile == 0 and tile % rows == 0

    def const(shape):
        nd = len(shape)
        return pl.BlockSpec(shape, lambda i, _nd=nd: (0,) * _nd, pipeline_mode=pl.Buffered(1))

    return pl.pallas_call(
        functools.partial(_channel_kernel, tile=tile, rows=rows),
        grid=(n_tok // tile,),
        in_specs=[
            pl.BlockSpec((tile, D_MODEL), lambda i: (i, 0)),
            const((1, D_MODEL)),
            const((D_MODEL, ROUTER_LANES)),
            const((N_GROUPS, D_MODEL, GROUP_WIDTH)),
            const((N_GROUPS, D_MODEL, GROUP_WIDTH)),
            const((N_GROUPS, GROUP_WIDTH, D_MODEL)),
            const((1, D_MODEL)),
        ],
        out_specs=pl.BlockSpec((tile, D_MODEL), lambda i: (i, 0)),
        out_shape=jax.ShapeDtypeStruct((n_tok, D_MODEL), F32),
        scratch_shapes=[
            pltpu.VMEM((tile, D_MODEL), BF16),
            pltpu.VMEM((tile, ROUTER_LANES), F32),
            pltpu.VMEM((tile, D_MODEL), F32),
        ],
        compiler_params=pltpu.CompilerParams(
            dimension_semantics=("arbitrary",), vmem_limit_bytes=VMEM_LIMIT_BYTES),
        name=f"channel_t{tile}",
    )(h, wts["g_ffn"], wts["w_router"], wts["w_gate_e"], wts["w_up_e"], wts["w_down_e"], wts["g_final"])


def _rope_tables(pos):
    inv_freq = ROPE_THETA ** (-jnp.arange(0, HEAD_DIM, 2, dtype=F32) / HEAD_DIM)
    ang = pos.astype(F32)[:, None] * inv_freq[None, :]
    cos = jnp.cos(ang)
    sin = jnp.sin(ang)
    reps = LANES // HEAD_DIM
    return jnp.tile(jnp.concatenate([cos, cos], axis=1), (1, reps)), jnp.tile(jnp.concatenate([-sin, sin], axis=1), (1, reps))


def _block_diag_gates(w_r, w_i):
    per = MXU_DIM // LRU_BLOCK_DIM

    def bd(w):
        w = w.reshape(GATE_BLOCKS, per, LRU_BLOCK_DIM, LRU_BLOCK_DIM)
        eye = jnp.eye(per, dtype=w.dtype)
        full = jnp.einsum("bpde,pq->bpdqe", w, eye)
        return full.reshape(GATE_BLOCKS, MXU_DIM, MXU_DIM)

    return jnp.concatenate([bd(w_r), bd(w_i)], axis=-1).astype(BF16)


def _group_columns(w):
    w = w.astype(BF16).reshape(N_GROUPS, EXPERTS_PER_GROUP, D_MODEL, D_EXPERT)
    return jnp.transpose(w, (0, 2, 1, 3)).reshape(N_GROUPS, D_MODEL, GROUP_WIDTH)


def _pad_meta(t, chunk):
    return jnp.pad(t, ((0, KEYS_PAD - WINDOW - chunk - N_META), (0, 0)))


def kernel(x_prompt, x_sample, cache_attn_k, cache_attn_v, state_conv, state_lru, meta_tokens,
           norm_mix_g, w_in, attn_sinks, w_attn_o, conv_w, conv_b, w_rg_r, b_rg_r, w_rg_i, b_rg_i,
           lru_lambda, w_rec_o, w_out, norm_ffn_g, w_group_router, w_expert_router, w_gate_e, w_up_e,
           w_down_e, norm_final_g):
    n_b, seq, _ = x_prompt.shape
    n_db, dseq, _ = x_sample.shape
    past_len = 2048
    layer = 0
    row = lambda t: t.reshape(1, -1)
    w_router = jnp.concatenate([w_expert_router[layer], w_group_router[layer]], axis=1)
    w_router = jnp.pad(w_router, ((0, 0), (0, ROUTER_LANES - N_EXPERTS - N_GROUPS)))
    wts = dict(
        sinks=attn_sinks[layer], g_mix=row(norm_mix_g[layer]), w_in=w_in[layer].astype(BF16),
        w_attn_o=w_attn_o[layer].astype(BF16), w_rec_o=w_rec_o[layer].astype(BF16), w_out=w_out[layer].astype(BF16),
        conv_w=conv_w[layer], conv_b=row(conv_b[layer]), w_gate=_block_diag_gates(w_rg_r[layer], w_rg_i[layer]),
        b_r=row(b_rg_r[layer]), b_i=row(b_rg_i[layer]), lam=row(lru_lambda[layer]),
        g_ffn=row(norm_ffn_g[layer]), w_router=w_router.astype(BF16), w_gate_e=_group_columns(w_gate_e[layer]),
        w_up_e=_group_columns(w_up_e[layer]),
        w_down_e=w_down_e[layer].astype(BF16).reshape(N_GROUPS, GROUP_WIDTH, D_MODEL), g_final=row(norm_final_g),
    )

    zeros_win = jnp.zeros((1, WINDOW, KV_WIDTH), F32)
    zeros_conv = jnp.zeros((1, SUBLANES, LRU_WIDTH), F32)
    zeros_h = jnp.zeros((1, 1, LRU_WIDTH), F32)

    cos_m, sin_m = _rope_tables(jnp.arange(N_META, dtype=jnp.int32))
    zero_meta = jnp.zeros((KEYS_PAD - WINDOW - N_META, KV_WIDTH), F32)
    _, k_m, v_m, conv_m, h_m = _mixer(
        meta_tokens[None], cos_m, sin_m, zeros_win, zeros_win, zeros_conv, zeros_h, zero_meta, zero_meta, wts,
        tile=N_META, chunk=N_META, hist0=0, meta_valid=False)
    k_meta = k_m[0, WINDOW - N_META:]
    v_meta = v_m[0, WINDOW - N_META:]

    tile_p = 256 if seq % 256 == 0 else CHUNK
    cos_p, sin_p = _rope_tables(N_META + jnp.arange(seq, dtype=jnp.int32))
    h_p, k_p, v_p, conv_p, lru_p = _mixer(
        x_prompt, cos_p, sin_p, zeros_win, zeros_win, conv_m, h_m, _pad_meta(k_meta, CHUNK), _pad_meta(v_meta, CHUNK),
        wts, tile=tile_p, chunk=CHUNK, hist0=0, meta_valid=True)

    cos_s, sin_s = _rope_tables(N_META + past_len + jnp.arange(dseq, dtype=jnp.int32))
    conv_s0 = jnp.pad(state_conv[layer], ((0, 0), (SUBLANES - (CONV_WIDTH - 1), 0), (0, 0)))
    h_s, k_s, v_s, conv_s, lru_s = _mixer(
        x_sample, cos_s, sin_s, cache_attn_k[layer].reshape(n_db, WINDOW, KV_WIDTH),
        cache_attn_v[layer].reshape(n_db, WINDOW, KV_WIDTH), conv_s0, state_lru[layer][:, None, :],
        _pad_meta(k_meta, dseq), _pad_meta(v_meta, dseq), wts, tile=dseq, chunk=dseq, hist0=WINDOW, meta_valid=True)

    tok_p = n_b * seq
    tok_s = n_db * dseq
    y_p = _channel(h_p.reshape(tok_p, D_MODEL), wts, tile=512 if tok_p % 512 == 0 else CHUNK)
    y_s = _channel(h_s.reshape(tok_s, D_MODEL), wts, tile=tok_s)

    keep = SUBLANES - (CONV_WIDTH - 1)
    kv_shape = lambda t: t.reshape(1, t.shape[0], WINDOW, N_KV_HEADS, HEAD_DIM)
    return (y_p.reshape(n_b, seq, D_MODEL), y_s.reshape(n_db, dseq, D_MODEL),
            kv_shape(k_p), kv_shape(v_p), conv_p[None, :, keep:, :], lru_p.reshape(1, n_b, LRU_WIDTH),
            kv_shape(k_s), kv_shape(v_s), conv_s[None, :, keep:, :], lru_s.reshape(1, n_db, LRU_WIDTH))
```

```python
import functools
import math

import jax
import jax.numpy as jnp
from jax import lax
from jax.experimental import pallas as pl
from jax.experimental.pallas import tpu as pltpu

D_MODEL = 1024
CHUNK = 64
N_META = 16
N_HEADS = 16
N_KV_HEADS = 2
HEAD_DIM = 64
HALF_DIM = HEAD_DIM // 2
Q_PER_KV = N_HEADS // N_KV_HEADS
ATTN_WIDTH = N_HEADS * HEAD_DIM
KV_WIDTH = N_KV_HEADS * HEAD_DIM
WINDOW = 128
ROPE_THETA = 10000.0
LRU_WIDTH = D_MODEL
LRU_BLOCKS = 16
LRU_BLOCK_DIM = LRU_WIDTH // LRU_BLOCKS
LRU_C = 8.0
CONV_WIDTH = 4
N_GROUPS = 4
EXPERTS_PER_GROUP = 4
N_EXPERTS = N_GROUPS * EXPERTS_PER_GROUP
D_EXPERT = 256
RMS_EPS = 1e-6
NEG_INF = -1e30
Q_END = ATTN_WIDTH
K_END = Q_END + KV_WIDTH
V_END = K_END + KV_WIDTH
XR_END = V_END + LRU_WIDTH
GR_END = XR_END + LRU_WIDTH
GA_END = GR_END + D_MODEL
IN_WIDTH = GA_END + D_MODEL
LOG2E = math.log2(math.e)

SUBLANES = 8
LANES = 128
MXU_DIM = 256
KEYS_PAD = 256
GATE_BLOCKS = LRU_WIDTH // MXU_DIM
ROUTER_LANES = LANES
EXPERT_ROWS = 128
VMEM_LIMIT_BYTES = 56 * 1024 * 1024

BF16 = jnp.bfloat16
F32 = jnp.float32


def _rmsnorm(x, g):
    ms = jnp.mean(x * x, axis=-1, keepdims=True)
    return (x * lax.rsqrt(ms + RMS_EPS)) * g


def _sigmoid(x):
    return 0.5 * jnp.tanh(0.5 * x) + 0.5


def _rope_slab(x, cos, sin_signed, first_half):
    partner = jnp.where(first_half, pltpu.roll(x, LANES - HALF_DIM, 1), pltpu.roll(x, HALF_DIM, 1))
    return x * cos + partner * sin_signed


def _dot(a, b):
    return jnp.dot(a, b, preferred_element_type=F32)


def _lru_scan(a, b, h0):
    t_rows, width = a.shape
    row = lax.broadcasted_iota(jnp.int32, (SUBLANES, width), 0)
    h = h0
    hs = []
    for g in range(t_rows // SUBLANES):
        ag = a[g * SUBLANES:(g + 1) * SUBLANES, :]
        bg = b[g * SUBLANES:(g + 1) * SUBLANES, :]
        for s in (1, 2, 4):
            keep = row >= s
            a_s = jnp.where(keep, pltpu.roll(ag, s, 0), 1.0)
            b_s = jnp.where(keep, pltpu.roll(bg, s, 0), 0.0)
            bg = ag * b_s + bg
            ag = ag * a_s
        hg = ag * h + bg
        hs.append(hg)
        h = hg[SUBLANES - 1:SUBLANES, :]
    return jnp.concatenate(hs, axis=0), h


def _mixer_kernel(sinks_ref, x_ref, cos_ref, sin_ref, kwin0_ref, vwin0_ref, conv0_ref, h0_ref,
                  kmeta_ref, vmeta_ref, gmix_ref, win_ref, wao_ref, wro_ref, wout_ref,
                  convw_ref, convb_ref, wgate_ref, br_ref, bi_ref, lam_ref,
                  hres_ref, knew_ref, vnew_ref, convnew_ref, lrunew_ref,
                  kbuf, vbuf, xrbuf, hcar, mbuf,
                  *, tile, chunk, hist0, meta_valid):
    s = pl.program_id(1)
    n_s = pl.num_programs(1)

    @pl.when(s == 0)
    def _init():
        kbuf[0:WINDOW, :] = kwin0_ref[0]
        vbuf[0:WINDOW, :] = vwin0_ref[0]
        xrbuf[0:SUBLANES, :] = conv0_ref[0]
        hcar[...] = h0_ref[0]

    x = x_ref[0]
    nb = _rmsnorm(x, gmix_ref[...]).astype(BF16)

    lane = lax.broadcasted_iota(jnp.int32, (1, LANES), 1)
    first_half = (lane & (HEAD_DIM - 1)) < HALF_DIM
    low_head = lane < HEAD_DIM
    cos = cos_ref[...]
    sin = sin_ref[...]

    q = _dot(nb, win_ref[:, 0:Q_END]) * (LOG2E * HEAD_DIM ** -0.5)
    kv = _dot(nb, win_ref[:, Q_END:V_END])
    xr = _dot(nb, win_ref[:, V_END:XR_END])
    kbuf[WINDOW:WINDOW + tile, :] = _rope_slab(kv[:, 0:KV_WIDTH], cos, sin, first_half)
    vbuf[WINDOW:WINDOW + tile, :] = kv[:, KV_WIDTH:2 * KV_WIDTH]
    xrbuf[SUBLANES:SUBLANES + tile, :] = xr
    qslabs = [
        _rope_slab(q[:, j * LANES:(j + 1) * LANES], cos, sin, first_half).astype(BF16)
        for j in range(ATTN_WIDTH // LANES)
    ]

    xr_ext = xrbuf[...]
    z = xr_ext * convw_ref[0:1, :]
    for tap in range(1, CONV_WIDTH):
        z = pltpu.roll(z, 1, 0) + xr_ext * convw_ref[tap:tap + 1, :]
    u = z[SUBLANES:, :] + convb_ref[...]
    ub = u.astype(BF16)
    r_parts = []
    i_parts = []
    for blk in range(GATE_BLOCKS):
        ri = _dot(ub[:, blk * MXU_DIM:(blk + 1) * MXU_DIM], wgate_ref[blk])
        r_parts.append(ri[:, 0:MXU_DIM])
        i_parts.append(ri[:, MXU_DIM:2 * MXU_DIM])

    col = lax.broadcasted_iota(jnp.int32, (1, KEYS_PAD), 1)
    pairs_per_kv = Q_PER_KV // 2
    rows = pairs_per_kv * chunk
    slab_of_row = lax.broadcasted_iota(jnp.int32, (rows, 1), 0) // chunk
    n_chunks = tile // chunk
    scores = []
    vpairs = []
    valids = []
    for ci in range(n_chunks):
        r0 = ci * chunk
        hist = jnp.minimum(WINDOW, hist0 + s * tile + r0)
        valid = ((col < WINDOW) & (col >= WINDOW - hist)) | ((col >= WINDOW) & (col < WINDOW + chunk))
        if meta_valid:
            valid = valid | ((col >= WINDOW + chunk) & (col < WINDOW + chunk + N_META))
        valids.append(valid)
        kcat = jnp.concatenate([kbuf[r0:r0 + WINDOW + chunk, :], kmeta_ref[...]], axis=0)
        vcat = jnp.concatenate([vbuf[r0:r0 + WINDOW + chunk, :], vmeta_ref[...]], axis=0)
        kswap = pltpu.roll(kcat, HEAD_DIM, 1)
        vswap = pltpu.roll(vcat, HEAD_DIM, 1)
        for g in range(N_KV_HEADS):
            k_lo, k_hi = (kcat, kswap) if g == 0 else (kswap, kcat)
            v_lo, v_hi = (vcat, vswap) if g == 0 else (vswap, vcat)
            kpair = jnp.concatenate([jnp.where(low_head, k_lo, 0.0), jnp.where(low_head, 0.0, k_hi)], axis=0)
            vpair = jnp.concatenate([jnp.where(low_head, v_lo, 0.0), jnp.where(low_head, 0.0, v_hi)], axis=0)
            vpairs.append(vpair.astype(BF16))
            qg = jnp.concatenate(
                [qslabs[g * pairs_per_kv + p][r0:r0 + chunk, :] for p in range(pairs_per_kv)], axis=0)
            scores.append(lax.dot_general(qg, kpair.astype(BF16), (((1,), (1,)), ((), ())),
                                          preferred_element_type=F32))

    mbuf[0] = _dot(nb, win_ref[:, XR_END:GR_END])
    mbuf[1] = _dot(nb, win_ref[:, GR_END:GA_END])
    mbuf[2] = _dot(nb, win_ref[:, GA_END:IN_WIDTH])

    r = _sigmoid(jnp.concatenate(r_parts, axis=1) + br_ref[...])
    i = _sigmoid(jnp.concatenate(i_parts, axis=1) + bi_ref[...])
    neg_lam = -lam_ref[...]
    softplus = jnp.maximum(neg_lam, 0.0) + jnp.log1p(jnp.exp(-jnp.abs(neg_lam)))
    log_a = (-LRU_C * r) * softplus
    a = jnp.exp(log_a)
    th = jnp.tanh(log_a)
    num = -2.0 * th
    root_num = jnp.where(num > 0.0, num * lax.rsqrt(num), 0.0)
    b = (root_num * lax.rsqrt(1.0 - th)) * (i * u)
    h, h_last = _lru_scan(a, b, hcar[...])

    ya_chunks = []
    for ci in range(n_chunks):
        slabs = [None] * (N_KV_HEADS * pairs_per_kv)
        for g in range(N_KV_HEADS):
            sc = scores[ci * N_KV_HEADS + g]
            e_parts = []
            inv_parts = []
            for half in range(2):
                sink = jnp.zeros((rows, 1), F32)
                for p in range(pairs_per_kv):
                    sink = jnp.where(slab_of_row == p, sinks_ref[g * Q_PER_KV + 2 * p + half] * LOG2E, sink)
                sh = jnp.where(valids[ci], sc[:, half * KEYS_PAD:(half + 1) * KEYS_PAD], NEG_INF)
                m = jnp.maximum(jnp.max(sh, axis=-1, keepdims=True), sink)
                e = jnp.exp2(sh - m)
                denom = jnp.sum(e, axis=-1, keepdims=True) + jnp.exp2(sink - m)
                e_parts.append(e.astype(BF16))
                inv_parts.append(1.0 / denom)
            o = _dot(jnp.concatenate(e_parts, axis=1), vpairs[ci * N_KV_HEADS + g])
            o = o * jnp.where(low_head, inv_parts[0], inv_parts[1])
            for p in range(pairs_per_kv):
                slabs[g * pairs_per_kv + p] = o[p * chunk:(p + 1) * chunk, :]
        ya_chunks.append(jnp.concatenate(slabs, axis=1))
    ya = jnp.concatenate(ya_chunks, axis=0).astype(BF16)

    t_attn = _dot(ya, wao_ref[...])
    yb = (h * jax.nn.gelu(mbuf[0])).astype(BF16)
    t_rec = _dot(yb, wro_ref[...])
    mix = (_sigmoid(mbuf[1]) * t_attn + _sigmoid(mbuf[2]) * t_rec).astype(BF16)
    hres_ref[0] = x + _dot(mix, wout_ref[...])

    knext = kbuf[tile:tile + WINDOW, :]
    vnext = vbuf[tile:tile + WINDOW, :]
    xrnext = xrbuf[tile:tile + SUBLANES, :]
    kbuf[0:WINDOW, :] = knext
    vbuf[0:WINDOW, :] = vnext
    xrbuf[0:SUBLANES, :] = xrnext
    hcar[...] = h_last

    @pl.when(s == n_s - 1)
    def _emit_state():
        knew_ref[0] = knext
        vnew_ref[0] = vnext
        convnew_ref[0] = xrnext
        lrunew_ref[0] = h_last


def _const_spec(shape):
    nd = len(shape)
    return pl.BlockSpec(shape, lambda b, s, _nd=nd: (0,) * _nd, pipeline_mode=pl.Buffered(1))


def _mixer(x, cos, sin, kwin0, vwin0, conv0, h0, kmeta, vmeta, wts, *, tile, chunk, hist0, meta_valid):
    n_b, seq, _ = x.shape
    assert seq % tile == 0 and tile % chunk == 0 and tile % SUBLANES == 0
    assert kmeta.shape == (KEYS_PAD - WINDOW - chunk, KV_WIDTH)
    per_batch_state = kwin0.shape[0] == n_b

    def state_map(b, s):
        return (b if per_batch_state else 0, 0, 0)

    in_specs = [
        pl.BlockSpec(memory_space=pltpu.SMEM),
        pl.BlockSpec((1, tile, D_MODEL), lambda b, s: (b, s, 0)),
        pl.BlockSpec((tile, LANES), lambda b, s: (s, 0)),
        pl.BlockSpec((tile, LANES), lambda b, s: (s, 0)),
        pl.BlockSpec((1, WINDOW, KV_WIDTH), state_map),
        pl.BlockSpec((1, WINDOW, KV_WIDTH), state_map),
        pl.BlockSpec((1, SUBLANES, LRU_WIDTH), state_map),
        pl.BlockSpec((1, 1, LRU_WIDTH), state_map),
        _const_spec(kmeta.shape), _const_spec(vmeta.shape),
        _const_spec((1, D_MODEL)),
        _const_spec((D_MODEL, IN_WIDTH)),
        _const_spec((ATTN_WIDTH, D_MODEL)), _const_spec((LRU_WIDTH, D_MODEL)), _const_spec((D_MODEL, D_MODEL)),
        _const_spec((CONV_WIDTH, LRU_WIDTH)), _const_spec((1, LRU_WIDTH)),
        _const_spec((GATE_BLOCKS, MXU_DIM, 2 * MXU_DIM)),
        _const_spec((1, LRU_WIDTH)), _const_spec((1, LRU_WIDTH)), _const_spec((1, LRU_WIDTH)),
    ]
    out_shape = (
        jax.ShapeDtypeStruct((n_b, seq, D_MODEL), F32),
        jax.ShapeDtypeStruct((n_b, WINDOW, KV_WIDTH), F32),
        jax.ShapeDtypeStruct((n_b, WINDOW, KV_WIDTH), F32),
        jax.ShapeDtypeStruct((n_b, SUBLANES, LRU_WIDTH), F32),
        jax.ShapeDtypeStruct((n_b, 1, LRU_WIDTH), F32),
    )
    out_specs = (
        pl.BlockSpec((1, tile, D_MODEL), lambda b, s: (b, s, 0)),
        pl.BlockSpec((1, WINDOW, KV_WIDTH), lambda b, s: (b, 0, 0)),
        pl.BlockSpec((1, WINDOW, KV_WIDTH), lambda b, s: (b, 0, 0)),
        pl.BlockSpec((1, SUBLANES, LRU_WIDTH), lambda b, s: (b, 0, 0)),
        pl.BlockSpec((1, 1, LRU_WIDTH), lambda b, s: (b, 0, 0)),
    )
    scratch = [
        pltpu.VMEM((WINDOW + tile, KV_WIDTH), F32),
        pltpu.VMEM((WINDOW + tile, KV_WIDTH), F32),
        pltpu.VMEM((SUBLANES + tile, LRU_WIDTH), F32),
        pltpu.VMEM((1, LRU_WIDTH), F32),
        pltpu.VMEM((3, tile, D_MODEL), F32),
    ]
    kern = functools.partial(_mixer_kernel, tile=tile, chunk=chunk, hist0=hist0, meta_valid=meta_valid)
    return pl.pallas_call(
        kern,
        grid=(n_b, seq // tile),
        in_specs=in_specs,
        out_specs=out_specs,
        out_shape=out_shape,
        scratch_shapes=scratch,
        compiler_params=pltpu.CompilerParams(
            dimension_semantics=("arbitrary", "arbitrary"), vmem_limit_bytes=VMEM_LIMIT_BYTES),
        name=f"mixer_t{tile}_c{chunk}",
    )(wts["sinks"], x, cos, sin, kwin0, vwin0, conv0, h0, kmeta, vmeta, wts["g_mix"], wts["w_in"],
      wts["w_attn_o"], wts["w_rec_o"], wts["w_out"], wts["conv_w"], wts["conv_b"], wts["w_gate"],
      wts["b_r"], wts["b_i"], wts["lam"])


def _split_bf16(v, parts):
    out = []
    for _ in range(parts):
        p = v.astype(BF16)
        out.append(p)
        v = v - p.astype(F32)
    return out


def _channel_kernel(h_ref, gffn_ref, wr_ref, wg_ref, wu_ref, wd_ref, gfin_ref, y_ref, xs, cs, acc, *, tile, rows):
    h = h_ref[...]
    nb = _rmsnorm(h, gffn_ref[...]).astype(BF16)
    logits = _dot(nb, wr_ref[...])
    lane = lax.broadcasted_iota(jnp.int32, logits.shape, 1)
    neg = -jnp.inf
    big = jnp.int32(ROUTER_LANES)

    gmask = (lane >= N_EXPERTS) & (lane < N_EXPERTS + N_GROUPS)
    gl = jnp.where(gmask, logits, neg)
    gmax = jnp.max(gl, axis=-1, keepdims=True)
    gidx = jnp.min(jnp.where(gl == gmax, lane - N_EXPERTS, big), axis=-1, keepdims=True)
    g_w = 1.0 / jnp.sum(jnp.where(gmask, jnp.exp(gl - gmax), 0.0), axis=-1, keepdims=True)

    emask = (lane < N_EXPERTS) & ((lane // EXPERTS_PER_GROUP) == gidx)
    el = jnp.where(emask, logits, neg)
    v1 = jnp.max(el, axis=-1, keepdims=True)
    i1 = jnp.min(jnp.where(el == v1, lane, big), axis=-1, keepdims=True)
    el2 = jnp.where(lane == i1, neg, el)
    v2 = jnp.max(el2, axis=-1, keepdims=True)
    i2 = jnp.min(jnp.where(el2 == v2, lane, big), axis=-1, keepdims=True)
    e2 = jnp.exp(v2 - v1)
    den = 1.0 + e2
    w1 = (1.0 / den) * g_w
    w2 = (e2 / den) * g_w
    comb = jnp.where(lane == i1, w1, 0.0) + jnp.where(lane == i2, w2, 0.0)

    g_col = gidx.astype(F32)
    g_row = jnp.transpose(jnp.broadcast_to(g_col, (tile, LANES)))[0:1, :]
    t_col = lax.broadcasted_iota(jnp.int32, (tile, 1), 0)
    t_row = lax.broadcasted_iota(jnp.int32, (1, tile), 1)
    a_before_b = (g_col < g_row) | ((g_col == g_row) & (t_col < t_row))
    pos_row = jnp.sum(a_before_b.astype(F32), axis=0, keepdims=True)
    pos_col = jnp.transpose(jnp.broadcast_to(pos_row, (LANES, tile)))[:, 0:1]
    perm = (pos_row == t_col.astype(F32)).astype(BF16)
    perm_t = (pos_col == t_row.astype(F32)).astype(BF16)

    xs[...] = _dot(perm, nb).astype(BF16)
    comb_parts = _split_bf16(comb, 3)
    cs[...] = (_dot(perm, comb_parts[0]) + _dot(perm, comb_parts[1])) + _dot(perm, comb_parts[2])

    start = [0.0] + [jnp.sum((g_col < float(g)).astype(F32)) for g in range(1, N_GROUPS)] + [float(tile)]

    acc[...] = jnp.zeros_like(acc)
    for c in range(tile // rows):
        blk = pl.ds(c * rows, rows)
        for g in range(N_GROUPS):
            @pl.when((start[g] < float((c + 1) * rows)) & (start[g + 1] > float(c * rows)))
            def _experts(blk=blk, g=g):
                xc = xs[blk, :]
                cc = cs[blk, :]
                experts = range(g * EXPERTS_PER_GROUP, (g + 1) * EXPERTS_PER_GROUP)
                hgs = [_dot(xc, wg_ref[e]) for e in experts]
                hus = [_dot(xc, wu_ref[e]) for e in experts]
                out = None
                for e, hg, hu in zip(experts, hgs, hus):
                    act = ((hg * _sigmoid(hg)) * hu) * cc[:, e:e + 1]
                    down = _dot(act.astype(BF16), wd_ref[e])
                    out = down if out is None else out + down
                acc[blk, :] += out

    moe_parts = _split_bf16(acc[...], 2)
    moe = _dot(perm_t, moe_parts[0]) + _dot(perm_t, moe_parts[1])
    y_ref[...] = _rmsnorm(h + moe, gfin_ref[...])


def _channel(h, wts, *, tile):
    n_tok = h.shape[0]
    rows = min(tile, EXPERT_ROWS)
    assert n_tok % tile == 0 and tile % rows == 0

    def const(shape):
        nd = len(shape)
        return pl.BlockSpec(shape, lambda i, _nd=nd: (0,) * _nd, pipeline_mode=pl.Buffered(1))

    return pl.pallas_call(
        functools.partial(_channel_kernel, tile=tile, rows=rows),
        grid=(n_tok // tile,),
        in_specs=[
            pl.BlockSpec((tile, D_MODEL), lambda i: (i, 0)),
            const((1, D_MODEL)),
            const((D_MODEL, ROUTER_LANES)),
            const((N_EXPERTS, D_MODEL, D_EXPERT)),
            const((N_EXPERTS, D_MODEL, D_EXPERT)),
            const((N_EXPERTS, D_EXPERT, D_MODEL)),
            const((1, D_MODEL)),
        ],
        out_specs=pl.BlockSpec((tile, D_MODEL), lambda i: (i, 0)),
        out_shape=jax.ShapeDtypeStruct((n_tok, D_MODEL), F32),
        scratch_shapes=[
            pltpu.VMEM((tile, D_MODEL), BF16),
            pltpu.VMEM((tile, ROUTER_LANES), F32),
            pltpu.VMEM((tile, D_MODEL), F32),
        ],
        compiler_params=pltpu.CompilerParams(
            dimension_semantics=("arbitrary",), vmem_limit_bytes=VMEM_LIMIT_BYTES),
        name=f"channel_t{tile}",
    )(h, wts["g_ffn"], wts["w_router"], wts["w_gate_e"], wts["w_up_e"], wts["w_down_e"], wts["g_final"])


def _rope_tables(pos):
    inv_freq = ROPE_THETA ** (-jnp.arange(0, HEAD_DIM, 2, dtype=F32) / HEAD_DIM)
    ang = pos.astype(F32)[:, None] * inv_freq[None, :]
    cos = jnp.cos(ang)
    sin = jnp.sin(ang)
    reps = LANES // HEAD_DIM
    return jnp.tile(jnp.concatenate([cos, cos], axis=1), (1, reps)), jnp.tile(jnp.concatenate([-sin, sin], axis=1), (1, reps))


def _block_diag_gates(w_r, w_i):
    per = MXU_DIM // LRU_BLOCK_DIM

    def bd(w):
        w = w.reshape(GATE_BLOCKS, per, LRU_BLOCK_DIM, LRU_BLOCK_DIM)
        eye = jnp.eye(per, dtype=w.dtype)
        full = jnp.einsum("bpde,pq->bpdqe", w, eye)
        return full.reshape(GATE_BLOCKS, MXU_DIM, MXU_DIM)

    return jnp.concatenate([bd(w_r), bd(w_i)], axis=-1).astype(BF16)


def _pad_meta(t, chunk):
    return jnp.pad(t, ((0, KEYS_PAD - WINDOW - chunk - N_META), (0, 0)))


def kernel(x_prompt, x_sample, cache_attn_k, cache_attn_v, state_conv, state_lru, meta_tokens,
           norm_mix_g, w_in, attn_sinks, w_attn_o, conv_w, conv_b, w_rg_r, b_rg_r, w_rg_i, b_rg_i,
           lru_lambda, w_rec_o, w_out, norm_ffn_g, w_group_router, w_expert_router, w_gate_e, w_up_e,
           w_down_e, norm_final_g):
    n_b, seq, _ = x_prompt.shape
    n_db, dseq, _ = x_sample.shape
    past_len = 2048
    layer = 0
    row = lambda t: t.reshape(1, -1)
    w_router = jnp.concatenate([w_expert_router[layer], w_group_router[layer]], axis=1)
    w_router = jnp.pad(w_router, ((0, 0), (0, ROUTER_LANES - N_EXPERTS - N_GROUPS)))
    wts = dict(
        sinks=attn_sinks[layer], g_mix=row(norm_mix_g[layer]), w_in=w_in[layer].astype(BF16),
        w_attn_o=w_attn_o[layer].astype(BF16), w_rec_o=w_rec_o[layer].astype(BF16), w_out=w_out[layer].astype(BF16),
        conv_w=conv_w[layer], conv_b=row(conv_b[layer]), w_gate=_block_diag_gates(w_rg_r[layer], w_rg_i[layer]),
        b_r=row(b_rg_r[layer]), b_i=row(b_rg_i[layer]), lam=row(lru_lambda[layer]),
        g_ffn=row(norm_ffn_g[layer]), w_router=w_router.astype(BF16), w_gate_e=w_gate_e[layer].astype(BF16),
        w_up_e=w_up_e[layer].astype(BF16), w_down_e=w_down_e[layer].astype(BF16), g_final=row(norm_final_g),
    )

    zeros_win = jnp.zeros((1, WINDOW, KV_WIDTH), F32)
    zeros_conv = jnp.zeros((1, SUBLANES, LRU_WIDTH), F32)
    zeros_h = jnp.zeros((1, 1, LRU_WIDTH), F32)

    cos_m, sin_m = _rope_tables(jnp.arange(N_META, dtype=jnp.int32))
    zero_meta = jnp.zeros((KEYS_PAD - WINDOW - N_META, KV_WIDTH), F32)
    _, k_m, v_m, conv_m, h_m = _mixer(
        meta_tokens[None], cos_m, sin_m, zeros_win, zeros_win, zeros_conv, zeros_h, zero_meta, zero_meta, wts,
        tile=N_META, chunk=N_META, hist0=0, meta_valid=False)
    k_meta = k_m[0, WINDOW - N_META:]
    v_meta = v_m[0, WINDOW - N_META:]

    tile_p = 512 if seq % 512 == 0 else CHUNK
    cos_p, sin_p = _rope_tables(N_META + jnp.arange(seq, dtype=jnp.int32))
    h_p, k_p, v_p, conv_p, lru_p = _mixer(
        x_prompt, cos_p, sin_p, zeros_win, zeros_win, conv_m, h_m, _pad_meta(k_meta, CHUNK), _pad_meta(v_meta, CHUNK),
        wts, tile=tile_p, chunk=CHUNK, hist0=0, meta_valid=True)

    cos_s, sin_s = _rope_tables(N_META + past_len + jnp.arange(dseq, dtype=jnp.int32))
    conv_s0 = jnp.pad(state_conv[layer], ((0, 0), (SUBLANES - (CONV_WIDTH - 1), 0), (0, 0)))
    h_s, k_s, v_s, conv_s, lru_s = _mixer(
        x_sample, cos_s, sin_s, cache_attn_k[layer].reshape(n_db, WINDOW, KV_WIDTH),
        cache_attn_v[layer].reshape(n_db, WINDOW, KV_WIDTH), conv_s0, state_lru[layer][:, None, :],
        _pad_meta(k_meta, dseq), _pad_meta(v_meta, dseq), wts, tile=dseq, chunk=dseq, hist0=WINDOW, meta_valid=True)

    tok_p = n_b * seq
    tok_s = n_db * dseq
    y_p = _channel(h_p.reshape(tok_p, D_MODEL), wts, tile=512 if tok_p % 512 == 0 else CHUNK)
    y_s = _channel(h_s.reshape(tok_s, D_MODEL), wts, tile=tok_s)

    keep = SUBLANES - (CONV_WIDTH - 1)
    kv_shape = lambda t: t.reshape(1, t.shape[0], WINDOW, N_KV_HEADS, HEAD_DIM)
    return (y_p.reshape(n_b, seq, D_MODEL), y_s.reshape(n_db, dseq, D_MODEL),
            kv_shape(k_p), kv_shape(v_p), conv_p[None, :, keep:, :], lru_p.reshape(1, n_b, LRU_WIDTH),
            kv_shape(k_s), kv_shape(v_s), conv_s[None, :, keep:, :], lru_s.reshape(1, n_db, LRU_WIDTH))
```

```python
import functools
import math

import jax
import jax.numpy as jnp
from jax import lax
from jax.experimental import pallas as pl
from jax.experimental.pallas import tpu as pltpu

D_MODEL = 1024
CHUNK = 64
N_META = 16
N_HEADS = 16
N_KV_HEADS = 2
HEAD_DIM = 64
HALF_DIM = HEAD_DIM // 2
Q_PER_KV = N_HEADS // N_KV_HEADS
ATTN_WIDTH = N_HEADS * HEAD_DIM
KV_WIDTH = N_KV_HEADS * HEAD_DIM
WINDOW = 128
ROPE_THETA = 10000.0
LRU_WIDTH = D_MODEL
LRU_BLOCKS = 16
LRU_BLOCK_DIM = LRU_WIDTH // LRU_BLOCKS
LRU_C = 8.0
CONV_WIDTH = 4
N_GROUPS = 4
EXPERTS_PER_GROUP = 4
N_EXPERTS = N_GROUPS * EXPERTS_PER_GROUP
D_EXPERT = 256
RMS_EPS = 1e-6
NEG_INF = -1e30
Q_END = ATTN_WIDTH
K_END = Q_END + KV_WIDTH
V_END = K_END + KV_WIDTH
XR_END = V_END + LRU_WIDTH
GR_END = XR_END + LRU_WIDTH
GA_END = GR_END + D_MODEL
IN_WIDTH = GA_END + D_MODEL
LOG2E = math.log2(math.e)

SUBLANES = 8
LANES = 128
MXU_DIM = 256
KEYS_PAD = 256
GATE_BLOCKS = LRU_WIDTH // MXU_DIM
ROUTER_ROWS = 32
ROUTER_LANES = LANES
EXPERT_ROWS = 128
VMEM_LIMIT_BYTES = 56 * 1024 * 1024

BF16 = jnp.bfloat16
F32 = jnp.float32


def _rmsnorm(x, g):
    ms = jnp.mean(x * x, axis=-1, keepdims=True)
    return (x * lax.rsqrt(ms + RMS_EPS)) * g


def _sigmoid(x):
    return 0.5 * jnp.tanh(0.5 * x) + 0.5


def _rope_slab(x, cos, sin_signed, first_half):
    partner = jnp.where(first_half, pltpu.roll(x, LANES - HALF_DIM, 1), pltpu.roll(x, HALF_DIM, 1))
    return x * cos + partner * sin_signed


def _dot(a, b):
    return jnp.dot(a, b, preferred_element_type=F32)


def _lru_scan(a, b, h0s):
    t_rows, width = a.shape
    groups_per_stream = t_rows // SUBLANES // len(h0s)
    row = lax.broadcasted_iota(jnp.int32, (SUBLANES, width), 0)
    hs = []
    lasts = []
    for g in range(t_rows // SUBLANES):
        if g % groups_per_stream == 0:
            h = h0s[g // groups_per_stream]
        ag = a[g * SUBLANES:(g + 1) * SUBLANES, :]
        bg = b[g * SUBLANES:(g + 1) * SUBLANES, :]
        for s in (1, 2, 4):
            keep = row >= s
            a_s = jnp.where(keep, pltpu.roll(ag, s, 0), 1.0)
            b_s = jnp.where(keep, pltpu.roll(bg, s, 0), 0.0)
            bg = ag * b_s + bg
            ag = ag * a_s
        hg = ag * h + bg
        hs.append(hg)
        h = hg[SUBLANES - 1:SUBLANES, :]
        if (g + 1) % groups_per_stream == 0:
            lasts.append(h)
    return jnp.concatenate(hs, axis=0), lasts


def _mixer_kernel(sinks_ref, x_ref, cos_ref, sin_ref, kwin0_ref, vwin0_ref, conv0_ref, h0_ref,
                  kmeta_ref, vmeta_ref, gmix_ref, win_ref, wao_ref, wro_ref, wout_ref,
                  convw_ref, convb_ref, wgate_ref, br_ref, bi_ref, lam_ref,
                  hres_ref, knew_ref, vnew_ref, convnew_ref, lrunew_ref,
                  kbuf, vbuf, xrbuf, hcar, mbuf,
                  *, tile, chunk, hist0, meta_valid, streams):
    s = pl.program_id(1)
    n_s = pl.num_programs(1)
    multi = streams > 1
    assert not multi or tile == streams * chunk

    if not multi:
        @pl.when(s == 0)
        def _init():
            kbuf[0:WINDOW, :] = kwin0_ref[0]
            vbuf[0:WINDOW, :] = vwin0_ref[0]
            xrbuf[0:SUBLANES, :] = conv0_ref[0]
            hcar[...] = h0_ref[0]

    x = x_ref[0]
    nb = _rmsnorm(x, gmix_ref[...]).astype(BF16)

    lane = lax.broadcasted_iota(jnp.int32, (1, LANES), 1)
    first_half = (lane & (HEAD_DIM - 1)) < HALF_DIM
    low_head = lane < HEAD_DIM
    cos = cos_ref[...]
    sin = sin_ref[...]

    q = _dot(nb, win_ref[:, 0:Q_END]) * (LOG2E * HEAD_DIM ** -0.5)
    kv = _dot(nb, win_ref[:, Q_END:V_END])
    xr = _dot(nb, win_ref[:, V_END:XR_END])
    k_new = _rope_slab(kv[:, 0:KV_WIDTH], cos, sin, first_half)
    v_new = kv[:, KV_WIDTH:2 * KV_WIDTH]
    if not multi:
        kbuf[WINDOW:WINDOW + tile, :] = k_new
        vbuf[WINDOW:WINDOW + tile, :] = v_new
        xrbuf[SUBLANES:SUBLANES + tile, :] = xr
    qslabs = [
        _rope_slab(q[:, j * LANES:(j + 1) * LANES], cos, sin, first_half).astype(BF16)
        for j in range(ATTN_WIDTH // LANES)
    ]

    def conv(xr_ext):
        z = xr_ext * convw_ref[0:1, :]
        for tap in range(1, CONV_WIDTH):
            z = pltpu.roll(z, 1, 0) + xr_ext * convw_ref[tap:tap + 1, :]
        return z[SUBLANES:, :] + convb_ref[...]

    if multi:
        xr_exts = [jnp.concatenate([conv0_ref[j], xr[j * chunk:(j + 1) * chunk, :]], axis=0) for j in range(streams)]
        u = jnp.concatenate([conv(ext) for ext in xr_exts], axis=0)
    else:
        u = conv(xrbuf[...])
    ub = u.astype(BF16)
    r_parts = []
    i_parts = []
    for blk in range(GATE_BLOCKS):
        ri = _dot(ub[:, blk * MXU_DIM:(blk + 1) * MXU_DIM], wgate_ref[blk])
        r_parts.append(ri[:, 0:MXU_DIM])
        i_parts.append(ri[:, MXU_DIM:2 * MXU_DIM])

    col = lax.broadcasted_iota(jnp.int32, (1, KEYS_PAD), 1)
    pairs_per_kv = Q_PER_KV // 2
    rows = pairs_per_kv * chunk
    slab_of_row = lax.broadcasted_iota(jnp.int32, (rows, 1), 0) // chunk
    n_chunks = tile // chunk
    scores = []
    vpairs = []
    valids = []
    for ci in range(n_chunks):
        r0 = ci * chunk
        hist = hist0 if multi else jnp.minimum(WINDOW, hist0 + s * tile + r0)
        valid = ((col < WINDOW) & (col >= WINDOW - hist)) | ((col >= WINDOW) & (col < WINDOW + chunk))
        if meta_valid:
            valid = valid | ((col >= WINDOW + chunk) & (col < WINDOW + chunk + N_META))
        valids.append(valid)
        if multi:
            kcat = jnp.concatenate([kwin0_ref[ci], k_new[r0:r0 + chunk, :], kmeta_ref[...]], axis=0)
            vcat = jnp.concatenate([vwin0_ref[ci], v_new[r0:r0 + chunk, :], vmeta_ref[...]], axis=0)
        else:
            kcat = jnp.concatenate([kbuf[r0:r0 + WINDOW + chunk, :], kmeta_ref[...]], axis=0)
            vcat = jnp.concatenate([vbuf[r0:r0 + WINDOW + chunk, :], vmeta_ref[...]], axis=0)
        kswap = pltpu.roll(kcat, HEAD_DIM, 1)
        vswap = pltpu.roll(vcat, HEAD_DIM, 1)
        for g in range(N_KV_HEADS):
            k_lo, k_hi = (kcat, kswap) if g == 0 else (kswap, kcat)
            v_lo, v_hi = (vcat, vswap) if g == 0 else (vswap, vcat)
            kpair = jnp.concatenate([jnp.where(low_head, k_lo, 0.0), jnp.where(low_head, 0.0, k_hi)], axis=0)
            vpair = jnp.concatenate([jnp.where(low_head, v_lo, 0.0), jnp.where(low_head, 0.0, v_hi)], axis=0)
            vpairs.append(vpair.astype(BF16))
            qg = jnp.concatenate(
                [qslabs[g * pairs_per_kv + p][r0:r0 + chunk, :] for p in range(pairs_per_kv)], axis=0)
            scores.append(lax.dot_general(qg, kpair.astype(BF16), (((1,), (1,)), ((), ())),
                                          preferred_element_type=F32))

    mbuf[0] = _dot(nb, win_ref[:, XR_END:GR_END])
    mbuf[1] = _dot(nb, win_ref[:, GR_END:GA_END])
    mbuf[2] = _dot(nb, win_ref[:, GA_END:IN_WIDTH])

    r = _sigmoid(jnp.concatenate(r_parts, axis=1) + br_ref[...])
    i = _sigmoid(jnp.concatenate(i_parts, axis=1) + bi_ref[...])
    neg_lam = -lam_ref[...]
    softplus = jnp.maximum(neg_lam, 0.0) + jnp.log1p(jnp.exp(-jnp.abs(neg_lam)))
    log_a = (-LRU_C * r) * softplus
    a = jnp.exp(log_a)
    th = jnp.tanh(log_a)
    num = -2.0 * th
    root_num = jnp.where(num > 0.0, num * lax.rsqrt(num), 0.0)
    b = (root_num * lax.rsqrt(1.0 - th)) * (i * u)
    h, h_lasts = _lru_scan(a, b, [h0_ref[j] for j in range(streams)] if multi else [hcar[...]])

    ya_chunks = []
    for ci in range(n_chunks):
        slabs = [None] * (N_KV_HEADS * pairs_per_kv)
        for g in range(N_KV_HEADS):
            sc = scores[ci * N_KV_HEADS + g]
            e_parts = []
            inv_parts = []
            for half in range(2):
                sink = jnp.zeros((rows, 1), F32)
                for p in range(pairs_per_kv):
                    sink = jnp.where(slab_of_row == p, sinks_ref[g * Q_PER_KV + 2 * p + half] * LOG2E, sink)
                sh = jnp.where(valids[ci], sc[:, half * KEYS_PAD:(half + 1) * KEYS_PAD], NEG_INF)
                m = jnp.maximum(jnp.max(sh, axis=-1, keepdims=True), sink)
                e = jnp.exp2(sh - m)
                denom = jnp.sum(e, axis=-1, keepdims=True) + jnp.exp2(sink - m)
                e_parts.append(e.astype(BF16))
                inv_parts.append(1.0 / denom)
            o = _dot(jnp.concatenate(e_parts, axis=1), vpairs[ci * N_KV_HEADS + g])
            o = o * jnp.where(low_head, inv_parts[0], inv_parts[1])
            for p in range(pairs_per_kv):
                slabs[g * pairs_per_kv + p] = o[p * chunk:(p + 1) * chunk, :]
        ya_chunks.append(jnp.concatenate(slabs, axis=1))
    ya = jnp.concatenate(ya_chunks, axis=0).astype(BF16)

    t_attn = _dot(ya, wao_ref[...])
    yb = (h * jax.nn.gelu(mbuf[0])).astype(BF16)
    t_rec = _dot(yb, wro_ref[...])
    mix = (_sigmoid(mbuf[1]) * t_attn + _sigmoid(mbuf[2]) * t_rec).astype(BF16)
    hres_ref[0] = x + _dot(mix, wout_ref[...])

    if multi:
        for j in range(streams):
            rows_j = slice(j * chunk, (j + 1) * chunk)
            knew_ref[j] = jnp.concatenate([kwin0_ref[j][chunk:, :], k_new[rows_j, :]], axis=0)
            vnew_ref[j] = jnp.concatenate([vwin0_ref[j][chunk:, :], v_new[rows_j, :]], axis=0)
            convnew_ref[j] = xr_exts[j][chunk:chunk + SUBLANES, :]
            lrunew_ref[j] = h_lasts[j]
    else:
        knext = kbuf[tile:tile + WINDOW, :]
        vnext = vbuf[tile:tile + WINDOW, :]
        xrnext = xrbuf[tile:tile + SUBLANES, :]
        kbuf[0:WINDOW, :] = knext
        vbuf[0:WINDOW, :] = vnext
        xrbuf[0:SUBLANES, :] = xrnext
        hcar[...] = h_lasts[0]

        @pl.when(s == n_s - 1)
        def _emit_state():
            knew_ref[0] = knext
            vnew_ref[0] = vnext
            convnew_ref[0] = xrnext
            lrunew_ref[0] = h_lasts[0]


def _const_spec(shape):
    nd = len(shape)
    return pl.BlockSpec(shape, lambda b, s, _nd=nd: (0,) * _nd, pipeline_mode=pl.Buffered(1))


def _mixer(x, cos, sin, kwin0, vwin0, conv0, h0, kmeta, vmeta, wts, *, tile, chunk, hist0, meta_valid, streams=1):
    n_b, seq, _ = x.shape
    assert seq % tile == 0 and tile % chunk == 0 and tile % SUBLANES == 0
    assert kmeta.shape == (KEYS_PAD - WINDOW - chunk, KV_WIDTH)
    multi = streams > 1
    assert not multi or (n_b == 1 and seq == tile == streams * chunk and kwin0.shape[0] == streams)
    n_state = streams if multi else n_b
    sb = streams if multi else 1
    per_batch_state = kwin0.shape[0] == n_b and not multi

    def state_map(b, s):
        return (b if per_batch_state else 0, 0, 0)

    in_specs = [
        pl.BlockSpec(memory_space=pltpu.SMEM),
        pl.BlockSpec((1, tile, D_MODEL), lambda b, s: (b, s, 0)),
        pl.BlockSpec((tile, LANES), lambda b, s: (s, 0)),
        pl.BlockSpec((tile, LANES), lambda b, s: (s, 0)),
        pl.BlockSpec((sb, WINDOW, KV_WIDTH), state_map),
        pl.BlockSpec((sb, WINDOW, KV_WIDTH), state_map),
        pl.BlockSpec((sb, SUBLANES, LRU_WIDTH), state_map),
        pl.BlockSpec((sb, 1, LRU_WIDTH), state_map),
        _const_spec(kmeta.shape), _const_spec(vmeta.shape),
        _const_spec((1, D_MODEL)),
        _const_spec((D_MODEL, IN_WIDTH)),
        _const_spec((ATTN_WIDTH, D_MODEL)), _const_spec((LRU_WIDTH, D_MODEL)), _const_spec((D_MODEL, D_MODEL)),
        _const_spec((CONV_WIDTH, LRU_WIDTH)), _const_spec((1, LRU_WIDTH)),
        _const_spec((GATE_BLOCKS, MXU_DIM, 2 * MXU_DIM)),
        _const_spec((1, LRU_WIDTH)), _const_spec((1, LRU_WIDTH)), _const_spec((1, LRU_WIDTH)),
    ]
    out_shape = (
        jax.ShapeDtypeStruct((n_b, seq, D_MODEL), F32),
        jax.ShapeDtypeStruct((n_state, WINDOW, KV_WIDTH), F32),
        jax.ShapeDtypeStruct((n_state, WINDOW, KV_WIDTH), F32),
        jax.ShapeDtypeStruct((n_state, SUBLANES, LRU_WIDTH), F32),
        jax.ShapeDtypeStruct((n_state, 1, LRU_WIDTH), F32),
    )
    out_specs = (
        pl.BlockSpec((1, tile, D_MODEL), lambda b, s: (b, s, 0)),
        pl.BlockSpec((sb, WINDOW, KV_WIDTH), lambda b, s: (b, 0, 0)),
        pl.BlockSpec((sb, WINDOW, KV_WIDTH), lambda b, s: (b, 0, 0)),
        pl.BlockSpec((sb, SUBLANES, LRU_WIDTH), lambda b, s: (b, 0, 0)),
        pl.BlockSpec((sb, 1, LRU_WIDTH), lambda b, s: (b, 0, 0)),
    )
    scratch = [
        pltpu.VMEM((WINDOW + tile, KV_WIDTH), F32),
        pltpu.VMEM((WINDOW + tile, KV_WIDTH), F32),
        pltpu.VMEM((SUBLANES + tile, LRU_WIDTH), F32),
        pltpu.VMEM((1, LRU_WIDTH), F32),
        pltpu.VMEM((3, tile, D_MODEL), F32),
    ]
    kern = functools.partial(_mixer_kernel, tile=tile, chunk=chunk, hist0=hist0, meta_valid=meta_valid,
                             streams=streams)
    return pl.pallas_call(
        kern,
        grid=(n_b, seq // tile),
        in_specs=in_specs,
        out_specs=out_specs,
        out_shape=out_shape,
        scratch_shapes=scratch,
        compiler_params=pltpu.CompilerParams(
            dimension_semantics=("arbitrary", "arbitrary"), vmem_limit_bytes=VMEM_LIMIT_BYTES),
        name=f"mixer_t{tile}_c{chunk}_s{streams}",
    )(wts["sinks"], x, cos, sin, kwin0, vwin0, conv0, h0, kmeta, vmeta, wts["g_mix"], wts["w_in"],
      wts["w_attn_o"], wts["w_rec_o"], wts["w_out"], wts["conv_w"], wts["conv_b"], wts["w_gate"],
      wts["b_r"], wts["b_i"], wts["lam"])


def _split_bf16(v, parts):
    out = []
    for _ in range(parts):
        p = v.astype(BF16)
        out.append(p)
        v = v - p.astype(F32)
    return out


def _channel_kernel(h_ref, gffn_ref, wr_ref, wg_ref, wu_ref, wd_ref, gfin_ref, y_ref, xs, cs, acc, *, tile, rows):
    h = h_ref[...]
    nb = _rmsnorm(h, gffn_ref[...]).astype(BF16)
    logits = lax.dot_general(wr_ref[...], nb, (((1,), (1,)), ((), ())), preferred_element_type=F32)
    neg = -jnp.inf
    big = float(ROUTER_ROWS)

    gl = logits[N_EXPERTS:N_EXPERTS + N_GROUPS, :]
    g_id = lax.broadcasted_iota(jnp.int32, gl.shape, 0).astype(F32)
    gmax = jnp.max(gl, axis=0, keepdims=True)
    g_row = jnp.min(jnp.where(gl == gmax, g_id, big), axis=0, keepdims=True)
    g_w = 1.0 / jnp.sum(jnp.exp(gl - gmax), axis=0, keepdims=True)

    e_id = lax.broadcasted_iota(jnp.int32, (N_EXPERTS, tile), 0)
    e_idf = e_id.astype(F32)
    emask = (e_id // EXPERTS_PER_GROUP).astype(F32) == g_row
    el = jnp.where(emask, logits[0:N_EXPERTS, :], neg)
    v1 = jnp.max(el, axis=0, keepdims=True)
    i1 = jnp.min(jnp.where(el == v1, e_idf, big), axis=0, keepdims=True)
    el2 = jnp.where(e_idf == i1, neg, el)
    v2 = jnp.max(el2, axis=0, keepdims=True)
    i2 = jnp.min(jnp.where(el2 == v2, e_idf, big), axis=0, keepdims=True)
    e2 = jnp.exp(v2 - v1)
    den = 1.0 + e2
    w1 = (1.0 / den) * g_w
    w2 = (e2 / den) * g_w
    comb = jnp.where(e_idf == i1, w1, 0.0) + jnp.where(e_idf == i2, w2, 0.0)

    g_col = jnp.transpose(jnp.broadcast_to(g_row, (LANES, tile)))[:, 0:1]
    t_col = lax.broadcasted_iota(jnp.int32, (tile, 1), 0)
    t_row = lax.broadcasted_iota(jnp.int32, (1, tile), 1)
    a_before_b = (g_col < g_row) | ((g_col == g_row) & (t_col < t_row))
    pos_row = jnp.sum(a_before_b.astype(F32), axis=0, keepdims=True)
    pos_col = jnp.transpose(jnp.broadcast_to(pos_row, (LANES, tile)))[:, 0:1]
    perm = (pos_row == t_col.astype(F32)).astype(BF16)
    perm_t = (pos_col == t_row.astype(F32)).astype(BF16)

    xs[...] = _dot(perm, nb).astype(BF16)
    comb_rows = jnp.concatenate([comb, jnp.zeros((ROUTER_LANES - N_EXPERTS, tile), F32)], axis=0)
    nt_dims = (((1,), (1,)), ((), ()))
    comb_parts = [lax.dot_general(perm, part, nt_dims, preferred_element_type=F32)
                  for part in _split_bf16(comb_rows, 3)]
    cs[...] = (comb_parts[0] + comb_parts[1]) + comb_parts[2]

    start = [0.0] + [jnp.sum((g_row < float(g)).astype(F32)) for g in range(1, N_GROUPS)] + [float(tile)]

    acc[...] = jnp.zeros_like(acc)
    for c in range(tile // rows):
        blk = pl.ds(c * rows, rows)
        for g in range(N_GROUPS):
            @pl.when((start[g] < float((c + 1) * rows)) & (start[g + 1] > float(c * rows)))
            def _experts(blk=blk, g=g):
                xc = xs[blk, :]
                cc = cs[blk, :]
                experts = range(g * EXPERTS_PER_GROUP, (g + 1) * EXPERTS_PER_GROUP)
                hgs = [_dot(xc, wg_ref[e]) for e in experts]
                hus = [_dot(xc, wu_ref[e]) for e in experts]
                out = None
                for e, hg, hu in zip(experts, hgs, hus):
                    act = ((hg * _sigmoid(hg)) * hu) * cc[:, e:e + 1]
                    down = _dot(act.astype(BF16), wd_ref[e])
                    out = down if out is None else out + down
                acc[blk, :] += out

    moe_parts = _split_bf16(acc[...], 2)
    moe = _dot(perm_t, moe_parts[0]) + _dot(perm_t, moe_parts[1])
    y_ref[...] = _rmsnorm(h + moe, gfin_ref[...])


def _channel(h, wts, *, tile):
    n_tok = h.shape[0]
    rows = min(tile, EXPERT_ROWS)
    assert n_tok % tile == 0 and tile % rows == 0

    def const(shape):
        nd = len(shape)
        return pl.BlockSpec(shape, lambda i, _nd=nd: (0,) * _nd, pipeline_mode=pl.Buffered(1))

    return pl.pallas_call(
        functools.partial(_channel_kernel, tile=tile, rows=rows),
        grid=(n_tok // tile,),
        in_specs=[
            pl.BlockSpec((tile, D_MODEL), lambda i: (i, 0)),
            const((1, D_MODEL)),
            const((ROUTER_ROWS, D_MODEL)),
            const((N_EXPERTS, D_MODEL, D_EXPERT)),
            const((N_EXPERTS, D_MODEL, D_EXPERT)),
            const((N_EXPERTS, D_EXPERT, D_MODEL)),
            const((1, D_MODEL)),
        ],
        out_specs=pl.BlockSpec((tile, D_MODEL), lambda i: (i, 0)),
        out_shape=jax.ShapeDtypeStruct((n_tok, D_MODEL), F32),
        scratch_shapes=[
            pltpu.VMEM((tile, D_MODEL), BF16),
            pltpu.VMEM((tile, ROUTER_LANES), F32),
            pltpu.VMEM((tile, D_MODEL), F32),
        ],
        compiler_params=pltpu.CompilerParams(
            dimension_semantics=("arbitrary",), vmem_limit_bytes=VMEM_LIMIT_BYTES),
        name=f"channel_t{tile}",
    )(h, wts["g_ffn"], wts["w_router"], wts["w_gate_e"], wts["w_up_e"], wts["w_down_e"], wts["g_final"])


def _rope_tables(pos):
    inv_freq = ROPE_THETA ** (-jnp.arange(0, HEAD_DIM, 2, dtype=F32) / HEAD_DIM)
    ang = pos.astype(F32)[:, None] * inv_freq[None, :]
    cos = jnp.cos(ang)
    sin = jnp.sin(ang)
    reps = LANES // HEAD_DIM
    return jnp.tile(jnp.concatenate([cos, cos], axis=1), (1, reps)), jnp.tile(jnp.concatenate([-sin, sin], axis=1), (1, reps))


def _block_diag_gates(w_r, w_i):
    per = MXU_DIM // LRU_BLOCK_DIM

    def bd(w):
        w = w.reshape(GATE_BLOCKS, per, LRU_BLOCK_DIM, LRU_BLOCK_DIM)
        eye = jnp.eye(per, dtype=w.dtype)
        full = jnp.einsum("bpde,pq->bpdqe", w, eye)
        return full.reshape(GATE_BLOCKS, MXU_DIM, MXU_DIM)

    return jnp.concatenate([bd(w_r), bd(w_i)], axis=-1).astype(BF16)


def _pad_meta(t, chunk):
    return jnp.pad(t, ((0, KEYS_PAD - WINDOW - chunk - N_META), (0, 0)))


def kernel(x_prompt, x_sample, cache_attn_k, cache_attn_v, state_conv, state_lru, meta_tokens,
           norm_mix_g, w_in, attn_sinks, w_attn_o, conv_w, conv_b, w_rg_r, b_rg_r, w_rg_i, b_rg_i,
           lru_lambda, w_rec_o, w_out, norm_ffn_g, w_group_router, w_expert_router, w_gate_e, w_up_e,
           w_down_e, norm_final_g):
    n_b, seq, _ = x_prompt.shape
    n_db, dseq, _ = x_sample.shape
    past_len = 2048
    layer = 0
    row = lambda t: t.reshape(1, -1)
    w_router = jnp.concatenate([w_expert_router[layer], w_group_router[layer]], axis=1).T
    w_router = jnp.pad(w_router, ((0, ROUTER_ROWS - N_EXPERTS - N_GROUPS), (0, 0)))
    wts = dict(
        sinks=attn_sinks[layer], g_mix=row(norm_mix_g[layer]), w_in=w_in[layer].astype(BF16),
        w_attn_o=w_attn_o[layer].astype(BF16), w_rec_o=w_rec_o[layer].astype(BF16), w_out=w_out[layer].astype(BF16),
        conv_w=conv_w[layer], conv_b=row(conv_b[layer]), w_gate=_block_diag_gates(w_rg_r[layer], w_rg_i[layer]),
        b_r=row(b_rg_r[layer]), b_i=row(b_rg_i[layer]), lam=row(lru_lambda[layer]),
        g_ffn=row(norm_ffn_g[layer]), w_router=w_router.astype(BF16), w_gate_e=w_gate_e[layer].astype(BF16),
        w_up_e=w_up_e[layer].astype(BF16), w_down_e=w_down_e[layer].astype(BF16), g_final=row(norm_final_g),
    )

    zeros_win = jnp.zeros((1, WINDOW, KV_WIDTH), F32)
    zeros_conv = jnp.zeros((1, SUBLANES, LRU_WIDTH), F32)
    zeros_h = jnp.zeros((1, 1, LRU_WIDTH), F32)

    cos_m, sin_m = _rope_tables(jnp.arange(N_META, dtype=jnp.int32))
    zero_meta = jnp.zeros((KEYS_PAD - WINDOW - N_META, KV_WIDTH), F32)
    _, k_m, v_m, conv_m, h_m = _mixer(
        meta_tokens[None], cos_m, sin_m, zeros_win, zeros_win, zeros_conv, zeros_h, zero_meta, zero_meta, wts,
        tile=N_META, chunk=N_META, hist0=0, meta_valid=False)
    k_meta = k_m[0, WINDOW - N_META:]
    v_meta = v_m[0, WINDOW - N_META:]

    tile_p = 512 if seq % 512 == 0 else CHUNK
    cos_p, sin_p = _rope_tables(N_META + jnp.arange(seq, dtype=jnp.int32))
    h_p, k_p, v_p, conv_p, lru_p = _mixer(
        x_prompt, cos_p, sin_p, zeros_win, zeros_win, conv_m, h_m, _pad_meta(k_meta, CHUNK), _pad_meta(v_meta, CHUNK),
        wts, tile=tile_p, chunk=CHUNK, hist0=0, meta_valid=True)

    cos_s, sin_s = _rope_tables(N_META + past_len + jnp.arange(dseq, dtype=jnp.int32))
    conv_s0 = jnp.pad(state_conv[layer], ((0, 0), (SUBLANES - (CONV_WIDTH - 1), 0), (0, 0)))
    h_s, k_s, v_s, conv_s, lru_s = _mixer(
        x_sample.reshape(1, n_db * dseq, D_MODEL), jnp.tile(cos_s, (n_db, 1)), jnp.tile(sin_s, (n_db, 1)),
        cache_attn_k[layer].reshape(n_db, WINDOW, KV_WIDTH), cache_attn_v[layer].reshape(n_db, WINDOW, KV_WIDTH),
        conv_s0, state_lru[layer][:, None, :], _pad_meta(k_meta, dseq), _pad_meta(v_meta, dseq), wts,
        tile=n_db * dseq, chunk=dseq, hist0=WINDOW, meta_valid=True, streams=n_db)

    tok_p = n_b * seq
    tok_s = n_db * dseq
    y_p = _channel(h_p.reshape(tok_p, D_MODEL), wts, tile=512 if tok_p % 512 == 0 else CHUNK)
    y_s = _channel(h_s.reshape(tok_s, D_MODEL), wts, tile=tok_s)

    keep = SUBLANES - (CONV_WIDTH - 1)
    kv_shape = lambda t: t.reshape(1, t.shape[0], WINDOW, N_KV_HEADS, HEAD_DIM)
    return (y_p.reshape(n_b, seq, D_MODEL), y_s.reshape(n_db, dseq, D_MODEL),
            kv_shape(k_p), kv_shape(v_p), conv_p[None, :, keep:, :], lru_p.reshape(1, n_b, LRU_WIDTH),
            kv_shape(k_s), kv_shape(v_s), conv_s[None, :, keep:, :], lru_s.reshape(1, n_db, LRU_WIDTH))
```

```python
import functools
import math

import jax
import jax.numpy as jnp
import numpy as np
from jax import lax
from jax.experimental import pallas as pl
from jax.experimental.pallas import tpu as pltpu

D_MODEL = 1024
CHUNK = 64
N_META = 16
N_HEADS = 16
N_KV_HEADS = 2
HEAD_DIM = 64
HALF_DIM = HEAD_DIM // 2
Q_PER_KV = N_HEADS // N_KV_HEADS
ATTN_WIDTH = N_HEADS * HEAD_DIM
KV_WIDTH = N_KV_HEADS * HEAD_DIM
WINDOW = 128
ROPE_THETA = 10000.0
LRU_WIDTH = D_MODEL
LRU_BLOCKS = 16
LRU_BLOCK_DIM = LRU_WIDTH // LRU_BLOCKS
LRU_C = 8.0
CONV_WIDTH = 4
N_GROUPS = 4
EXPERTS_PER_GROUP = 4
N_EXPERTS = N_GROUPS * EXPERTS_PER_GROUP
D_EXPERT = 256
RMS_EPS = 1e-6
NEG_INF = -1e30
Q_END = ATTN_WIDTH
K_END = Q_END + KV_WIDTH
V_END = K_END + KV_WIDTH
XR_END = V_END + LRU_WIDTH
GR_END = XR_END + LRU_WIDTH
GA_END = GR_END + D_MODEL
IN_WIDTH = GA_END + D_MODEL
LOG2E = math.log2(math.e)

SUBLANES = 8
LANES = 128
MXU_DIM = 256
KEYS_PAD = 256
GATE_BLOCKS = LRU_WIDTH // MXU_DIM
ROUTER_ROWS = 32
ROUTER_LANES = LANES
EXPERT_ROWS = 128
VMEM_LIMIT_BYTES = 56 * 1024 * 1024

BF16 = jnp.bfloat16
F32 = jnp.float32


def _rmsnorm(x, g):
    ms = jnp.mean(x * x, axis=-1, keepdims=True)
    return (x * lax.rsqrt(ms + RMS_EPS)) * g


def _sigmoid(x):
    return 0.5 * jnp.tanh(0.5 * x) + 0.5


def _rope_slab(x, cos, sin_signed, first_half):
    partner = jnp.where(first_half, pltpu.roll(x, LANES - HALF_DIM, 1), pltpu.roll(x, HALF_DIM, 1))
    return x * cos + partner * sin_signed


def _dot(a, b):
    return jnp.dot(a, b, preferred_element_type=F32)


def _lru_scan(a, b, h0s):
    t_rows, width = a.shape
    groups_per_stream = t_rows // SUBLANES // len(h0s)
    row = lax.broadcasted_iota(jnp.int32, (SUBLANES, width), 0)
    hs = []
    lasts = []
    for g in range(t_rows // SUBLANES):
        if g % groups_per_stream == 0:
            h = h0s[g // groups_per_stream]
        ag = a[g * SUBLANES:(g + 1) * SUBLANES, :]
        bg = b[g * SUBLANES:(g + 1) * SUBLANES, :]
        for s in (1, 2, 4):
            keep = row >= s
            a_s = jnp.where(keep, pltpu.roll(ag, s, 0), 1.0)
            b_s = jnp.where(keep, pltpu.roll(bg, s, 0), 0.0)
            bg = ag * b_s + bg
            ag = ag * a_s
        hg = ag * h + bg
        hs.append(hg)
        h = hg[SUBLANES - 1:SUBLANES, :]
        if (g + 1) % groups_per_stream == 0:
            lasts.append(h)
    return jnp.concatenate(hs, axis=0), lasts


def _mixer_kernel(sinks_ref, x_ref, cos_ref, sin_ref, kwin0_ref, vwin0_ref, conv0_ref, h0_ref,
                  kmeta_ref, vmeta_ref, gmix_ref, win_ref, wao_ref, wro_ref, wout_ref,
                  convw_ref, convb_ref, wgate_ref, br_ref, bi_ref, lam_ref,
                  hres_ref, knew_ref, vnew_ref, convnew_ref, lrunew_ref,
                  kbuf, vbuf, xrbuf, hcar, mbuf,
                  *, tile, chunk, hist0, meta_valid, streams):
    s = pl.program_id(1)
    n_s = pl.num_programs(1)
    multi = streams > 1
    assert not multi or tile == streams * chunk

    if not multi:
        @pl.when(s == 0)
        def _init():
            kbuf[0:WINDOW, :] = kwin0_ref[0]
            vbuf[0:WINDOW, :] = vwin0_ref[0]
            xrbuf[0:SUBLANES, :] = conv0_ref[0]
            hcar[...] = h0_ref[0]

    x = x_ref[0]
    nb = _rmsnorm(x, gmix_ref[...]).astype(BF16)

    lane = lax.broadcasted_iota(jnp.int32, (1, LANES), 1)
    first_half = (lane & (HEAD_DIM - 1)) < HALF_DIM
    low_head = lane < HEAD_DIM
    cos = cos_ref[...]
    sin = sin_ref[...]

    q = _dot(nb, win_ref[:, 0:Q_END]) * (LOG2E * HEAD_DIM ** -0.5)
    kv = _dot(nb, win_ref[:, Q_END:V_END])
    xr = _dot(nb, win_ref[:, V_END:XR_END])
    k_new = _rope_slab(kv[:, 0:KV_WIDTH], cos, sin, first_half)
    v_new = kv[:, KV_WIDTH:2 * KV_WIDTH]
    if not multi:
        kbuf[WINDOW:WINDOW + tile, :] = k_new
        vbuf[WINDOW:WINDOW + tile, :] = v_new
        xrbuf[SUBLANES:SUBLANES + tile, :] = xr
    qslabs = [
        _rope_slab(q[:, j * LANES:(j + 1) * LANES], cos, sin, first_half).astype(BF16)
        for j in range(ATTN_WIDTH // LANES)
    ]

    def conv(xr_ext):
        z = xr_ext * convw_ref[0:1, :]
        for tap in range(1, CONV_WIDTH):
            z = pltpu.roll(z, 1, 0) + xr_ext * convw_ref[tap:tap + 1, :]
        return z[SUBLANES:, :] + convb_ref[...]

    if multi:
        xr_exts = [jnp.concatenate([conv0_ref[j], xr[j * chunk:(j + 1) * chunk, :]], axis=0) for j in range(streams)]
        u = jnp.concatenate([conv(ext) for ext in xr_exts], axis=0)
    else:
        u = conv(xrbuf[...])
    ub = u.astype(BF16)
    r_parts = []
    i_parts = []
    for blk in range(GATE_BLOCKS):
        ri = _dot(ub[:, blk * MXU_DIM:(blk + 1) * MXU_DIM], wgate_ref[blk])
        r_parts.append(ri[:, 0:MXU_DIM])
        i_parts.append(ri[:, MXU_DIM:2 * MXU_DIM])

    col = lax.broadcasted_iota(jnp.int32, (1, KEYS_PAD), 1)
    pairs_per_kv = Q_PER_KV // 2
    rows = pairs_per_kv * chunk
    slab_of_row = lax.broadcasted_iota(jnp.int32, (rows, 1), 0) // chunk
    n_chunks = tile // chunk
    scores = []
    vpairs = []
    valids = []
    for ci in range(n_chunks):
        r0 = ci * chunk
        hist = hist0 if multi else jnp.minimum(WINDOW, hist0 + s * tile + r0)
        valid = ((col < WINDOW) & (col >= WINDOW - hist)) | ((col >= WINDOW) & (col < WINDOW + chunk))
        if meta_valid:
            valid = valid | ((col >= WINDOW + chunk) & (col < WINDOW + chunk + N_META))
        valids.append(valid)
        if multi:
            kcat = jnp.concatenate([kwin0_ref[ci], k_new[r0:r0 + chunk, :], kmeta_ref[...]], axis=0)
            vcat = jnp.concatenate([vwin0_ref[ci], v_new[r0:r0 + chunk, :], vmeta_ref[...]], axis=0)
        else:
            kcat = jnp.concatenate([kbuf[r0:r0 + WINDOW + chunk, :], kmeta_ref[...]], axis=0)
            vcat = jnp.concatenate([vbuf[r0:r0 + WINDOW + chunk, :], vmeta_ref[...]], axis=0)
        kswap = pltpu.roll(kcat, HEAD_DIM, 1)
        vswap = pltpu.roll(vcat, HEAD_DIM, 1)
        for g in range(N_KV_HEADS):
            k_lo, k_hi = (kcat, kswap) if g == 0 else (kswap, kcat)
            v_lo, v_hi = (vcat, vswap) if g == 0 else (vswap, vcat)
            kpair = jnp.concatenate([jnp.where(low_head, k_lo, 0.0), jnp.where(low_head, 0.0, k_hi)], axis=0)
            vpair = jnp.concatenate([jnp.where(low_head, v_lo, 0.0), jnp.where(low_head, 0.0, v_hi)], axis=0)
            vpairs.append(vpair.astype(BF16))
            qg = jnp.concatenate(
                [qslabs[g * pairs_per_kv + p][r0:r0 + chunk, :] for p in range(pairs_per_kv)], axis=0)
            scores.append(lax.dot_general(qg, kpair.astype(BF16), (((1,), (1,)), ((), ())),
                                          preferred_element_type=F32))

    mbuf[0] = _dot(nb, win_ref[:, XR_END:GR_END])
    mbuf[1] = _dot(nb, win_ref[:, GR_END:GA_END])
    mbuf[2] = _dot(nb, win_ref[:, GA_END:IN_WIDTH])

    r = _sigmoid(jnp.concatenate(r_parts, axis=1) + br_ref[...])
    i = _sigmoid(jnp.concatenate(i_parts, axis=1) + bi_ref[...])
    neg_lam = -lam_ref[...]
    softplus = jnp.maximum(neg_lam, 0.0) + jnp.log1p(jnp.exp(-jnp.abs(neg_lam)))
    log_a = (-LRU_C * r) * softplus
    a = jnp.exp(log_a)
    th = jnp.tanh(log_a)
    num = -2.0 * th
    root_num = jnp.where(num > 0.0, num * lax.rsqrt(num), 0.0)
    b = (root_num * lax.rsqrt(1.0 - th)) * (i * u)
    h, h_lasts = _lru_scan(a, b, [h0_ref[j] for j in range(streams)] if multi else [hcar[...]])

    ya_chunks = []
    for ci in range(n_chunks):
        slabs = [None] * (N_KV_HEADS * pairs_per_kv)
        for g in range(N_KV_HEADS):
            sc = scores[ci * N_KV_HEADS + g]
            e_parts = []
            inv_parts = []
            for half in range(2):
                sink = jnp.zeros((rows, 1), F32)
                for p in range(pairs_per_kv):
                    sink = jnp.where(slab_of_row == p, sinks_ref[g * Q_PER_KV + 2 * p + half] * LOG2E, sink)
                sh = jnp.where(valids[ci], sc[:, half * KEYS_PAD:(half + 1) * KEYS_PAD], NEG_INF)
                m = jnp.maximum(jnp.max(sh, axis=-1, keepdims=True), sink)
                e = jnp.exp2(sh - m)
                denom = jnp.sum(e, axis=-1, keepdims=True) + jnp.exp2(sink - m)
                e_parts.append(e.astype(BF16))
                inv_parts.append(1.0 / denom)
            o = _dot(jnp.concatenate(e_parts, axis=1), vpairs[ci * N_KV_HEADS + g])
            o = o * jnp.where(low_head, inv_parts[0], inv_parts[1])
            for p in range(pairs_per_kv):
                slabs[g * pairs_per_kv + p] = o[p * chunk:(p + 1) * chunk, :]
        ya_chunks.append(jnp.concatenate(slabs, axis=1))
    ya = jnp.concatenate(ya_chunks, axis=0).astype(BF16)

    t_attn = _dot(ya, wao_ref[...])
    yb = (h * jax.nn.gelu(mbuf[0])).astype(BF16)
    t_rec = _dot(yb, wro_ref[...])
    mix = (_sigmoid(mbuf[1]) * t_attn + _sigmoid(mbuf[2]) * t_rec).astype(BF16)
    hres_ref[0] = x + _dot(mix, wout_ref[...])

    if multi:
        for j in range(streams):
            rows_j = slice(j * chunk, (j + 1) * chunk)
            knew_ref[j] = jnp.concatenate([kwin0_ref[j][chunk:, :], k_new[rows_j, :]], axis=0)
            vnew_ref[j] = jnp.concatenate([vwin0_ref[j][chunk:, :], v_new[rows_j, :]], axis=0)
            convnew_ref[j] = xr_exts[j][chunk:chunk + SUBLANES, :]
            lrunew_ref[j] = h_lasts[j]
    else:
        knext = kbuf[tile:tile + WINDOW, :]
        vnext = vbuf[tile:tile + WINDOW, :]
        xrnext = xrbuf[tile:tile + SUBLANES, :]
        kbuf[0:WINDOW, :] = knext
        vbuf[0:WINDOW, :] = vnext
        xrbuf[0:SUBLANES, :] = xrnext
        hcar[...] = h_lasts[0]

        @pl.when(s == n_s - 1)
        def _emit_state():
            knew_ref[0] = knext
            vnew_ref[0] = vnext
            convnew_ref[0] = xrnext
            lrunew_ref[0] = h_lasts[0]


def _const_spec(shape):
    nd = len(shape)
    return pl.BlockSpec(shape, lambda b, s, _nd=nd: (0,) * _nd, pipeline_mode=pl.Buffered(1))


def _mixer(x, cos, sin, kwin0, vwin0, conv0, h0, kmeta, vmeta, wts, *, tile, chunk, hist0, meta_valid, streams=1):
    n_b, seq, _ = x.shape
    assert seq % tile == 0 and tile % chunk == 0 and tile % SUBLANES == 0
    assert kmeta.shape == (KEYS_PAD - WINDOW - chunk, KV_WIDTH)
    multi = streams > 1
    assert not multi or (n_b == 1 and seq == tile == streams * chunk and kwin0.shape[0] == streams)
    n_state = streams if multi else n_b
    sb = streams if multi else 1
    per_batch_state = kwin0.shape[0] == n_b and not multi

    def state_map(b, s):
        return (b if per_batch_state else 0, 0, 0)

    in_specs = [
        pl.BlockSpec(memory_space=pltpu.SMEM),
        pl.BlockSpec((1, tile, D_MODEL), lambda b, s: (b, s, 0)),
        pl.BlockSpec((tile, LANES), lambda b, s: (s, 0)),
        pl.BlockSpec((tile, LANES), lambda b, s: (s, 0)),
        pl.BlockSpec((sb, WINDOW, KV_WIDTH), state_map),
        pl.BlockSpec((sb, WINDOW, KV_WIDTH), state_map),
        pl.BlockSpec((sb, SUBLANES, LRU_WIDTH), state_map),
        pl.BlockSpec((sb, 1, LRU_WIDTH), state_map),
        _const_spec(kmeta.shape), _const_spec(vmeta.shape),
        _const_spec((1, D_MODEL)),
        _const_spec((D_MODEL, IN_WIDTH)),
        _const_spec((ATTN_WIDTH, D_MODEL)), _const_spec((LRU_WIDTH, D_MODEL)), _const_spec((D_MODEL, D_MODEL)),
        _const_spec((CONV_WIDTH, LRU_WIDTH)), _const_spec((1, LRU_WIDTH)),
        _const_spec((GATE_BLOCKS, MXU_DIM, 2 * MXU_DIM)),
        _const_spec((1, LRU_WIDTH)), _const_spec((1, LRU_WIDTH)), _const_spec((1, LRU_WIDTH)),
    ]
    out_shape = (
        jax.ShapeDtypeStruct((n_b, seq, D_MODEL), F32),
        jax.ShapeDtypeStruct((n_state, WINDOW, KV_WIDTH), F32),
        jax.ShapeDtypeStruct((n_state, WINDOW, KV_WIDTH), F32),
        jax.ShapeDtypeStruct((n_state, SUBLANES, LRU_WIDTH), F32),
        jax.ShapeDtypeStruct((n_state, 1, LRU_WIDTH), F32),
    )
    out_specs = (
        pl.BlockSpec((1, tile, D_MODEL), lambda b, s: (b, s, 0)),
        pl.BlockSpec((sb, WINDOW, KV_WIDTH), lambda b, s: (b, 0, 0)),
        pl.BlockSpec((sb, WINDOW, KV_WIDTH), lambda b, s: (b, 0, 0)),
        pl.BlockSpec((sb, SUBLANES, LRU_WIDTH), lambda b, s: (b, 0, 0)),
        pl.BlockSpec((sb, 1, LRU_WIDTH), lambda b, s: (b, 0, 0)),
    )
    scratch = [
        pltpu.VMEM((WINDOW + tile, KV_WIDTH), F32),
        pltpu.VMEM((WINDOW + tile, KV_WIDTH), F32),
        pltpu.VMEM((SUBLANES + tile, LRU_WIDTH), F32),
        pltpu.VMEM((1, LRU_WIDTH), F32),
        pltpu.VMEM((3, tile, D_MODEL), F32),
    ]
    kern = functools.partial(_mixer_kernel, tile=tile, chunk=chunk, hist0=hist0, meta_valid=meta_valid,
                             streams=streams)
    return pl.pallas_call(
        kern,
        grid=(n_b, seq // tile),
        in_specs=in_specs,
        out_specs=out_specs,
        out_shape=out_shape,
        scratch_shapes=scratch,
        compiler_params=pltpu.CompilerParams(
            dimension_semantics=("arbitrary", "arbitrary"), vmem_limit_bytes=VMEM_LIMIT_BYTES),
        name=f"mixer_t{tile}_c{chunk}_s{streams}",
    )(wts["sinks"], x, cos, sin, kwin0, vwin0, conv0, h0, kmeta, vmeta, wts["g_mix"], wts["w_in"],
      wts["w_attn_o"], wts["w_rec_o"], wts["w_out"], wts["conv_w"], wts["conv_b"], wts["w_gate"],
      wts["b_r"], wts["b_i"], wts["lam"])


def _split_bf16(v, parts):
    out = []
    for _ in range(parts):
        p = v.astype(BF16)
        out.append(p)
        v = v - p.astype(F32)
    return out


def _channel_kernel(h_ref, gffn_ref, wr_ref, wg_ref, wu_ref, wd_ref, gfin_ref, y_ref, xs, cs, acc, *, tile, rows):
    h = h_ref[...]
    nb = _rmsnorm(h, gffn_ref[...]).astype(BF16)
    logits = lax.dot_general(wr_ref[...], nb, (((1,), (1,)), ((), ())), preferred_element_type=F32)
    neg = -jnp.inf
    big = float(ROUTER_ROWS)

    gl = logits[N_EXPERTS:N_EXPERTS + N_GROUPS, :]
    g_id = lax.broadcasted_iota(jnp.int32, gl.shape, 0).astype(F32)
    gmax = jnp.max(gl, axis=0, keepdims=True)
    g_row = jnp.min(jnp.where(gl == gmax, g_id, big), axis=0, keepdims=True)
    g_w = 1.0 / jnp.sum(jnp.exp(gl - gmax), axis=0, keepdims=True)

    e_id = lax.broadcasted_iota(jnp.int32, (N_EXPERTS, tile), 0)
    e_idf = e_id.astype(F32)
    emask = (e_id // EXPERTS_PER_GROUP).astype(F32) == g_row
    el = jnp.where(emask, logits[0:N_EXPERTS, :], neg)
    v1 = jnp.max(el, axis=0, keepdims=True)
    i1 = jnp.min(jnp.where(el == v1, e_idf, big), axis=0, keepdims=True)
    el2 = jnp.where(e_idf == i1, neg, el)
    v2 = jnp.max(el2, axis=0, keepdims=True)
    i2 = jnp.min(jnp.where(el2 == v2, e_idf, big), axis=0, keepdims=True)
    e2 = jnp.exp(v2 - v1)
    den = 1.0 + e2
    w1 = (1.0 / den) * g_w
    w2 = (e2 / den) * g_w
    comb = jnp.where(e_idf == i1, w1, 0.0) + jnp.where(e_idf == i2, w2, 0.0)

    g_col = jnp.transpose(jnp.broadcast_to(g_row, (LANES, tile)))[:, 0:1]
    t_col = lax.broadcasted_iota(jnp.int32, (tile, 1), 0)
    t_row = lax.broadcasted_iota(jnp.int32, (1, tile), 1)
    a_before_b = (g_col < g_row) | ((g_col == g_row) & (t_col < t_row))
    pos_row = jnp.sum(a_before_b.astype(F32), axis=0, keepdims=True)
    pos_col = jnp.transpose(jnp.broadcast_to(pos_row, (LANES, tile)))[:, 0:1]
    perm = (pos_row == t_col.astype(F32)).astype(BF16)
    perm_t = (pos_col == t_row.astype(F32)).astype(BF16)

    xs[...] = _dot(perm, nb).astype(BF16)
    comb_rows = jnp.concatenate([comb, jnp.zeros((ROUTER_LANES - N_EXPERTS, tile), F32)], axis=0)
    nt_dims = (((1,), (1,)), ((), ()))
    comb_parts = [lax.dot_general(perm, part, nt_dims, preferred_element_type=F32)
                  for part in _split_bf16(comb_rows, 3)]
    cs[...] = (comb_parts[0] + comb_parts[1]) + comb_parts[2]

    start = [0.0] + [jnp.sum((g_row < float(g)).astype(F32)) for g in range(1, N_GROUPS)] + [float(tile)]

    acc[...] = jnp.zeros_like(acc)
    for c in range(tile // rows):
        blk = pl.ds(c * rows, rows)
        for g in range(N_GROUPS):
            @pl.when((start[g] < float((c + 1) * rows)) & (start[g + 1] > float(c * rows)))
            def _experts(blk=blk, g=g):
                xc = xs[blk, :]
                cc = cs[blk, :]
                experts = range(g * EXPERTS_PER_GROUP, (g + 1) * EXPERTS_PER_GROUP)
                hgs = [_dot(xc, wg_ref[e]) for e in experts]
                hus = [_dot(xc, wu_ref[e]) for e in experts]
                out = None
                for e, hg, hu in zip(experts, hgs, hus):
                    act = ((hg * _sigmoid(hg)) * hu) * cc[:, e:e + 1]
                    down = _dot(act.astype(BF16), wd_ref[e])
                    out = down if out is None else out + down
                acc[blk, :] += out

    moe_parts = _split_bf16(acc[...], 2)
    moe = _dot(perm_t, moe_parts[0]) + _dot(perm_t, moe_parts[1])
    y_ref[...] = _rmsnorm(h + moe, gfin_ref[...])


def _channel(h, wts, *, tile):
    n_tok = h.shape[0]
    rows = min(tile, EXPERT_ROWS)
    assert n_tok % tile == 0 and tile % rows == 0

    def const(shape):
        nd = len(shape)
        return pl.BlockSpec(shape, lambda i, _nd=nd: (0,) * _nd, pipeline_mode=pl.Buffered(1))

    return pl.pallas_call(
        functools.partial(_channel_kernel, tile=tile, rows=rows),
        grid=(n_tok // tile,),
        in_specs=[
            pl.BlockSpec((tile, D_MODEL), lambda i: (i, 0)),
            const((1, D_MODEL)),
            const((ROUTER_ROWS, D_MODEL)),
            const((N_EXPERTS, D_MODEL, D_EXPERT)),
            const((N_EXPERTS, D_MODEL, D_EXPERT)),
            const((N_EXPERTS, D_EXPERT, D_MODEL)),
            const((1, D_MODEL)),
        ],
        out_specs=pl.BlockSpec((tile, D_MODEL), lambda i: (i, 0)),
        out_shape=jax.ShapeDtypeStruct((n_tok, D_MODEL), F32),
        scratch_shapes=[
            pltpu.VMEM((tile, D_MODEL), BF16),
            pltpu.VMEM((tile, ROUTER_LANES), F32),
            pltpu.VMEM((tile, D_MODEL), F32),
        ],
        compiler_params=pltpu.CompilerParams(
            dimension_semantics=("arbitrary",), vmem_limit_bytes=VMEM_LIMIT_BYTES),
        name=f"channel_t{tile}",
    )(h, wts["g_ffn"], wts["w_router"], wts["w_gate_e"], wts["w_up_e"], wts["w_down_e"], wts["g_final"])


def _rope_tables(first, count):
    inv_freq = ROPE_THETA ** (-np.arange(0, HEAD_DIM, 2, dtype=np.float64) / HEAD_DIM)
    ang = np.arange(first, first + count, dtype=np.float64)[:, None] * inv_freq[None, :]
    cos = np.cos(ang).astype(np.float32)
    sin = np.sin(ang).astype(np.float32)
    reps = LANES // HEAD_DIM
    return (jnp.asarray(np.tile(np.concatenate([cos, cos], axis=1), (1, reps))),
            jnp.asarray(np.tile(np.concatenate([-sin, sin], axis=1), (1, reps))))


def _block_diag_gates(w_r, w_i):
    per = MXU_DIM // LRU_BLOCK_DIM

    def bd(w):
        w = w.reshape(GATE_BLOCKS, per, LRU_BLOCK_DIM, LRU_BLOCK_DIM)
        eye = jnp.eye(per, dtype=w.dtype)
        full = jnp.einsum("bpde,pq->bpdqe", w, eye)
        return full.reshape(GATE_BLOCKS, MXU_DIM, MXU_DIM)

    return jnp.concatenate([bd(w_r), bd(w_i)], axis=-1).astype(BF16)


def _pad_meta(t, chunk):
    return jnp.pad(t, ((0, KEYS_PAD - WINDOW - chunk - N_META), (0, 0)))


def kernel(x_prompt, x_sample, cache_attn_k, cache_attn_v, state_conv, state_lru, meta_tokens,
           norm_mix_g, w_in, attn_sinks, w_attn_o, conv_w, conv_b, w_rg_r, b_rg_r, w_rg_i, b_rg_i,
           lru_lambda, w_rec_o, w_out, norm_ffn_g, w_group_router, w_expert_router, w_gate_e, w_up_e,
           w_down_e, norm_final_g):
    n_b, seq, _ = x_prompt.shape
    n_db, dseq, _ = x_sample.shape
    past_len = 2048
    layer = 0
    row = lambda t: t.reshape(1, -1)
    w_router = jnp.concatenate([w_expert_router[layer], w_group_router[layer]], axis=1).T
    w_router = jnp.pad(w_router, ((0, ROUTER_ROWS - N_EXPERTS - N_GROUPS), (0, 0)))
    wts = dict(
        sinks=attn_sinks[layer], g_mix=row(norm_mix_g[layer]), w_in=w_in[layer].astype(BF16),
        w_attn_o=w_attn_o[layer].astype(BF16), w_rec_o=w_rec_o[layer].astype(BF16), w_out=w_out[layer].astype(BF16),
        conv_w=conv_w[layer], conv_b=row(conv_b[layer]), w_gate=_block_diag_gates(w_rg_r[layer], w_rg_i[layer]),
        b_r=row(b_rg_r[layer]), b_i=row(b_rg_i[layer]), lam=row(lru_lambda[layer]),
        g_ffn=row(norm_ffn_g[layer]), w_router=w_router.astype(BF16), w_gate_e=w_gate_e[layer].astype(BF16),
        w_up_e=w_up_e[layer].astype(BF16), w_down_e=w_down_e[layer].astype(BF16), g_final=row(norm_final_g),
    )

    zeros_win = jnp.zeros((1, WINDOW, KV_WIDTH), F32)
    zeros_conv = jnp.zeros((1, SUBLANES, LRU_WIDTH), F32)
    zeros_h = jnp.zeros((1, 1, LRU_WIDTH), F32)

    cos_m, sin_m = _rope_tables(0, N_META)
    zero_meta = jnp.zeros((KEYS_PAD - WINDOW - N_META, KV_WIDTH), F32)
    _, k_m, v_m, conv_m, h_m = _mixer(
        meta_tokens[None], cos_m, sin_m, zeros_win, zeros_win, zeros_conv, zeros_h, zero_meta, zero_meta, wts,
        tile=N_META, chunk=N_META, hist0=0, meta_valid=False)
    k_meta = k_m[0, WINDOW - N_META:]
    v_meta = v_m[0, WINDOW - N_META:]

    tile_p = 512 if seq % 512 == 0 else CHUNK
    cos_p, sin_p = _rope_tables(N_META, seq)
    h_p, k_p, v_p, conv_p, lru_p = _mixer(
        x_prompt, cos_p, sin_p, zeros_win, zeros_win, conv_m, h_m, _pad_meta(k_meta, CHUNK), _pad_meta(v_meta, CHUNK),
        wts, tile=tile_p, chunk=CHUNK, hist0=0, meta_valid=True)

    cos_s, sin_s = _rope_tables(N_META + past_len, dseq)
    conv_s0 = jnp.pad(state_conv[layer], ((0, 0), (SUBLANES - (CONV_WIDTH - 1), 0), (0, 0)))
    h_s, k_s, v_s, conv_s, lru_s = _mixer(
        x_sample.reshape(1, n_db * dseq, D_MODEL), jnp.tile(cos_s, (n_db, 1)), jnp.tile(sin_s, (n_db, 1)),
        cache_attn_k[layer].reshape(n_db, WINDOW, KV_WIDTH), cache_attn_v[layer].reshape(n_db, WINDOW, KV_WIDTH),
        conv_s0, state_lru[layer][:, None, :], _pad_meta(k_meta, dseq), _pad_meta(v_meta, dseq), wts,
        tile=n_db * dseq, chunk=dseq, hist0=WINDOW, meta_valid=True, streams=n_db)

    tok_p = n_b * seq
    tok_s = n_db * dseq
    y_p = _channel(h_p.reshape(tok_p, D_MODEL), wts, tile=512 if tok_p % 512 == 0 else CHUNK)
    y_s = _channel(h_s.reshape(tok_s, D_MODEL), wts, tile=tok_s)

    keep = SUBLANES - (CONV_WIDTH - 1)
    kv_shape = lambda t: t.reshape(1, t.shape[0], WINDOW, N_KV_HEADS, HEAD_DIM)
    return (y_p.reshape(n_b, seq, D_MODEL), y_s.reshape(n_db, dseq, D_MODEL),
            kv_shape(k_p), kv_shape(v_p), conv_p[None, :, keep:, :], lru_p.reshape(1, n_b, LRU_WIDTH),
            kv_shape(k_s), kv_shape(v_s), conv_s[None, :, keep:, :], lru_s.reshape(1, n_db, LRU_WIDTH))
```

```python
import functools
import math

import jax
import jax.numpy as jnp
import numpy as np
from jax import lax
from jax.experimental import pallas as pl
from jax.experimental.pallas import tpu as pltpu

D_MODEL = 1024
CHUNK = 64
N_META = 16
N_HEADS = 16
N_KV_HEADS = 2
HEAD_DIM = 64
HALF_DIM = HEAD_DIM // 2
Q_PER_KV = N_HEADS // N_KV_HEADS
ATTN_WIDTH = N_HEADS * HEAD_DIM
KV_WIDTH = N_KV_HEADS * HEAD_DIM
WINDOW = 128
ROPE_THETA = 10000.0
LRU_WIDTH = D_MODEL
LRU_BLOCKS = 16
LRU_BLOCK_DIM = LRU_WIDTH // LRU_BLOCKS
LRU_C = 8.0
CONV_WIDTH = 4
N_GROUPS = 4
EXPERTS_PER_GROUP = 4
N_EXPERTS = N_GROUPS * EXPERTS_PER_GROUP
D_EXPERT = 256
RMS_EPS = 1e-6
NEG_INF = -1e30
Q_END = ATTN_WIDTH
K_END = Q_END + KV_WIDTH
V_END = K_END + KV_WIDTH
XR_END = V_END + LRU_WIDTH
GR_END = XR_END + LRU_WIDTH
GA_END = GR_END + D_MODEL
IN_WIDTH = GA_END + D_MODEL
LOG2E = math.log2(math.e)

SUBLANES = 8
LANES = 128
MXU_DIM = 256
KEYS_PAD = 256
GATE_BLOCKS = LRU_WIDTH // MXU_DIM
ROUTER_ROWS = 32
ROUTER_LANES = LANES
EXPERT_ROWS = 128
COMB_PARTS = 3
VMEM_LIMIT_BYTES = 56 * 1024 * 1024

BF16 = jnp.bfloat16
F32 = jnp.float32


def _rmsnorm(x, g):
    ms = jnp.mean(x * x, axis=-1, keepdims=True)
    return (x * lax.rsqrt(ms + RMS_EPS)) * g


def _sigmoid(x):
    return 0.5 * jnp.tanh(0.5 * x) + 0.5


def _rope_slab(x, cos, sin_signed, first_half):
    partner = jnp.where(first_half, pltpu.roll(x, LANES - HALF_DIM, 1), pltpu.roll(x, HALF_DIM, 1))
    return x * cos + partner * sin_signed


def _dot(a, b):
    return jnp.dot(a, b, preferred_element_type=F32)


def _lru_scan(a, b, h0s):
    t_rows, width = a.shape
    groups_per_stream = t_rows // SUBLANES // len(h0s)
    row = lax.broadcasted_iota(jnp.int32, (SUBLANES, width), 0)
    hs = []
    lasts = []
    for g in range(t_rows // SUBLANES):
        if g % groups_per_stream == 0:
            h = h0s[g // groups_per_stream]
        ag = a[g * SUBLANES:(g + 1) * SUBLANES, :]
        bg = b[g * SUBLANES:(g + 1) * SUBLANES, :]
        for s in (1, 2, 4):
            keep = row >= s
            a_s = jnp.where(keep, pltpu.roll(ag, s, 0), 1.0)
            b_s = jnp.where(keep, pltpu.roll(bg, s, 0), 0.0)
            bg = ag * b_s + bg
            ag = ag * a_s
        hg = ag * h + bg
        hs.append(hg)
        h = hg[SUBLANES - 1:SUBLANES, :]
        if (g + 1) % groups_per_stream == 0:
            lasts.append(h)
    return jnp.concatenate(hs, axis=0), lasts


def _mixer_kernel(sinks_ref, x_ref, cos_ref, sin_ref, kwin0_ref, vwin0_ref, conv0_ref, h0_ref,
                  kmeta_ref, vmeta_ref, gmix_ref, win_ref, wao_ref, wro_ref, wout_ref,
                  convw_ref, convb_ref, wgate_ref, br_ref, bi_ref, lam_ref,
                  hres_ref, knew_ref, vnew_ref, convnew_ref, lrunew_ref,
                  kbuf, vbuf, xrbuf, hcar, mbuf,
                  *, tile, chunk, hist0, meta_valid, streams):
    s = pl.program_id(1)
    n_s = pl.num_programs(1)
    multi = streams > 1
    assert not multi or tile == streams * chunk

    if not multi:
        @pl.when(s == 0)
        def _init():
            kbuf[0:WINDOW, :] = kwin0_ref[0]
            vbuf[0:WINDOW, :] = vwin0_ref[0]
            xrbuf[0:SUBLANES, :] = conv0_ref[0]
            hcar[...] = h0_ref[0]

    x = x_ref[0]
    nb = _rmsnorm(x, gmix_ref[...]).astype(BF16)

    lane = lax.broadcasted_iota(jnp.int32, (1, LANES), 1)
    first_half = (lane & (HEAD_DIM - 1)) < HALF_DIM
    low_head = lane < HEAD_DIM
    cos = cos_ref[...]
    sin = sin_ref[...]

    q = _dot(nb, win_ref[:, 0:Q_END]) * (LOG2E * HEAD_DIM ** -0.5)
    kv = _dot(nb, win_ref[:, Q_END:V_END])
    xr = _dot(nb, win_ref[:, V_END:XR_END])
    k_new = _rope_slab(kv[:, 0:KV_WIDTH], cos, sin, first_half)
    v_new = kv[:, KV_WIDTH:2 * KV_WIDTH]
    if not multi:
        kbuf[WINDOW:WINDOW + tile, :] = k_new
        vbuf[WINDOW:WINDOW + tile, :] = v_new
        xrbuf[SUBLANES:SUBLANES + tile, :] = xr
    qslabs = [
        _rope_slab(q[:, j * LANES:(j + 1) * LANES], cos, sin, first_half).astype(BF16)
        for j in range(ATTN_WIDTH // LANES)
    ]

    def conv(xr_ext):
        z = xr_ext * convw_ref[0:1, :]
        for tap in range(1, CONV_WIDTH):
            z = pltpu.roll(z, 1, 0) + xr_ext * convw_ref[tap:tap + 1, :]
        return z[SUBLANES:, :] + convb_ref[...]

    if multi:
        xr_exts = [jnp.concatenate([conv0_ref[j], xr[j * chunk:(j + 1) * chunk, :]], axis=0) for j in range(streams)]
        u = jnp.concatenate([conv(ext) for ext in xr_exts], axis=0)
    else:
        u = conv(xrbuf[...])
    ub = u.astype(BF16)
    r_parts = []
    i_parts = []
    for blk in range(GATE_BLOCKS):
        ri = _dot(ub[:, blk * MXU_DIM:(blk + 1) * MXU_DIM], wgate_ref[blk])
        r_parts.append(ri[:, 0:MXU_DIM])
        i_parts.append(ri[:, MXU_DIM:2 * MXU_DIM])

    col = lax.broadcasted_iota(jnp.int32, (1, KEYS_PAD), 1)
    pairs_per_kv = Q_PER_KV // 2
    rows = pairs_per_kv * chunk
    slab_of_row = lax.broadcasted_iota(jnp.int32, (rows, 1), 0) // chunk
    n_chunks = tile // chunk
    scores = []
    vpairs = []
    valids = []
    for ci in range(n_chunks):
        r0 = ci * chunk
        hist = hist0 if multi else jnp.minimum(WINDOW, hist0 + s * tile + r0)
        valid = ((col < WINDOW) & (col >= WINDOW - hist)) | ((col >= WINDOW) & (col < WINDOW + chunk))
        if meta_valid:
            valid = valid | ((col >= WINDOW + chunk) & (col < WINDOW + chunk + N_META))
        valids.append(valid)
        if multi:
            kcat = jnp.concatenate([kwin0_ref[ci], k_new[r0:r0 + chunk, :], kmeta_ref[...]], axis=0)
            vcat = jnp.concatenate([vwin0_ref[ci], v_new[r0:r0 + chunk, :], vmeta_ref[...]], axis=0)
        else:
            kcat = jnp.concatenate([kbuf[r0:r0 + WINDOW + chunk, :], kmeta_ref[...]], axis=0)
            vcat = jnp.concatenate([vbuf[r0:r0 + WINDOW + chunk, :], vmeta_ref[...]], axis=0)
        kswap = pltpu.roll(kcat, HEAD_DIM, 1)
        vswap = pltpu.roll(vcat, HEAD_DIM, 1)
        for g in range(N_KV_HEADS):
            k_lo, k_hi = (kcat, kswap) if g == 0 else (kswap, kcat)
            v_lo, v_hi = (vcat, vswap) if g == 0 else (vswap, vcat)
            kpair = jnp.concatenate([jnp.where(low_head, k_lo, 0.0), jnp.where(low_head, 0.0, k_hi)], axis=0)
            vpair = jnp.concatenate([jnp.where(low_head, v_lo, 0.0), jnp.where(low_head, 0.0, v_hi)], axis=0)
            vpairs.append(vpair.astype(BF16))
            qg = jnp.concatenate(
                [qslabs[g * pairs_per_kv + p][r0:r0 + chunk, :] for p in range(pairs_per_kv)], axis=0)
            scores.append(lax.dot_general(qg, kpair.astype(BF16), (((1,), (1,)), ((), ())),
                                          preferred_element_type=F32))

    mbuf[0] = _dot(nb, win_ref[:, XR_END:GR_END])
    mbuf[1] = _dot(nb, win_ref[:, GR_END:GA_END])
    mbuf[2] = _dot(nb, win_ref[:, GA_END:IN_WIDTH])

    r = _sigmoid(jnp.concatenate(r_parts, axis=1) + br_ref[...])
    i = _sigmoid(jnp.concatenate(i_parts, axis=1) + bi_ref[...])
    neg_lam = -lam_ref[...]
    softplus = jnp.maximum(neg_lam, 0.0) + jnp.log1p(jnp.exp(-jnp.abs(neg_lam)))
    log_a = (-LRU_C * r) * softplus
    a = jnp.exp(log_a)
    th = jnp.tanh(log_a)
    num = -2.0 * th
    root_num = jnp.where(num > 0.0, num * lax.rsqrt(num), 0.0)
    b = (root_num * lax.rsqrt(1.0 - th)) * (i * u)
    h, h_lasts = _lru_scan(a, b, [h0_ref[j] for j in range(streams)] if multi else [hcar[...]])

    ya_chunks = []
    for ci in range(n_chunks):
        slabs = [None] * (N_KV_HEADS * pairs_per_kv)
        for g in range(N_KV_HEADS):
            sc = scores[ci * N_KV_HEADS + g]
            e_parts = []
            inv_parts = []
            for half in range(2):
                sink = jnp.zeros((rows, 1), F32)
                for p in range(pairs_per_kv):
                    sink = jnp.where(slab_of_row == p, sinks_ref[g * Q_PER_KV + 2 * p + half] * LOG2E, sink)
                sh = jnp.where(valids[ci], sc[:, half * KEYS_PAD:(half + 1) * KEYS_PAD], NEG_INF)
                m = jnp.maximum(jnp.max(sh, axis=-1, keepdims=True), sink)
                e = jnp.exp2(sh - m)
                denom = jnp.sum(e, axis=-1, keepdims=True) + jnp.exp2(sink - m)
                e_parts.append(e.astype(BF16))
                inv_parts.append(1.0 / denom)
            o = _dot(jnp.concatenate(e_parts, axis=1), vpairs[ci * N_KV_HEADS + g])
            o = o * jnp.where(low_head, inv_parts[0], inv_parts[1])
            for p in range(pairs_per_kv):
                slabs[g * pairs_per_kv + p] = o[p * chunk:(p + 1) * chunk, :]
        ya_chunks.append(jnp.concatenate(slabs, axis=1))
    ya = jnp.concatenate(ya_chunks, axis=0).astype(BF16)

    t_attn = _dot(ya, wao_ref[...])
    yb = (h * jax.nn.gelu(mbuf[0])).astype(BF16)
    t_rec = _dot(yb, wro_ref[...])
    mix = (_sigmoid(mbuf[1]) * t_attn + _sigmoid(mbuf[2]) * t_rec).astype(BF16)
    hres_ref[0] = x + _dot(mix, wout_ref[...])

    if multi:
        for j in range(streams):
            rows_j = slice(j * chunk, (j + 1) * chunk)
            knew_ref[j] = jnp.concatenate([kwin0_ref[j][chunk:, :], k_new[rows_j, :]], axis=0)
            vnew_ref[j] = jnp.concatenate([vwin0_ref[j][chunk:, :], v_new[rows_j, :]], axis=0)
            convnew_ref[j] = xr_exts[j][chunk:chunk + SUBLANES, :]
            lrunew_ref[j] = h_lasts[j]
    else:
        knext = kbuf[tile:tile + WINDOW, :]
        vnext = vbuf[tile:tile + WINDOW, :]
        xrnext = xrbuf[tile:tile + SUBLANES, :]
        kbuf[0:WINDOW, :] = knext
        vbuf[0:WINDOW, :] = vnext
        xrbuf[0:SUBLANES, :] = xrnext
        hcar[...] = h_lasts[0]

        @pl.when(s == n_s - 1)
        def _emit_state():
            knew_ref[0] = knext
            vnew_ref[0] = vnext
            convnew_ref[0] = xrnext
            lrunew_ref[0] = h_lasts[0]


def _const_spec(shape):
    nd = len(shape)
    return pl.BlockSpec(shape, lambda b, s, _nd=nd: (0,) * _nd, pipeline_mode=pl.Buffered(1))


def _mixer(x, cos, sin, kwin0, vwin0, conv0, h0, kmeta, vmeta, wts, *, tile, chunk, hist0, meta_valid, streams=1):
    n_b, seq, _ = x.shape
    assert seq % tile == 0 and tile % chunk == 0 and tile % SUBLANES == 0
    assert kmeta.shape == (KEYS_PAD - WINDOW - chunk, KV_WIDTH)
    multi = streams > 1
    assert not multi or (n_b == 1 and seq == tile == streams * chunk and kwin0.shape[0] == streams)
    n_state = streams if multi else n_b
    sb = streams if multi else 1
    per_batch_state = kwin0.shape[0] == n_b and not multi

    def state_map(b, s):
        return (b if per_batch_state else 0, 0, 0)

    in_specs = [
        pl.BlockSpec(memory_space=pltpu.SMEM),
        pl.BlockSpec((1, tile, D_MODEL), lambda b, s: (b, s, 0)),
        pl.BlockSpec((tile, LANES), lambda b, s: (s, 0)),
        pl.BlockSpec((tile, LANES), lambda b, s: (s, 0)),
        pl.BlockSpec((sb, WINDOW, KV_WIDTH), state_map),
        pl.BlockSpec((sb, WINDOW, KV_WIDTH), state_map),
        pl.BlockSpec((sb, SUBLANES, LRU_WIDTH), state_map),
        pl.BlockSpec((sb, 1, LRU_WIDTH), state_map),
        _const_spec(kmeta.shape), _const_spec(vmeta.shape),
        _const_spec((1, D_MODEL)),
        _const_spec((D_MODEL, IN_WIDTH)),
        _const_spec((ATTN_WIDTH, D_MODEL)), _const_spec((LRU_WIDTH, D_MODEL)), _const_spec((D_MODEL, D_MODEL)),
        _const_spec((CONV_WIDTH, LRU_WIDTH)), _const_spec((1, LRU_WIDTH)),
        _const_spec((GATE_BLOCKS, MXU_DIM, 2 * MXU_DIM)),
        _const_spec((1, LRU_WIDTH)), _const_spec((1, LRU_WIDTH)), _const_spec((1, LRU_WIDTH)),
    ]
    out_shape = (
        jax.ShapeDtypeStruct((n_b, seq, D_MODEL), F32),
        jax.ShapeDtypeStruct((n_state, WINDOW, KV_WIDTH), F32),
        jax.ShapeDtypeStruct((n_state, WINDOW, KV_WIDTH), F32),
        jax.ShapeDtypeStruct((n_state, SUBLANES, LRU_WIDTH), F32),
        jax.ShapeDtypeStruct((n_state, 1, LRU_WIDTH), F32),
    )
    out_specs = (
        pl.BlockSpec((1, tile, D_MODEL), lambda b, s: (b, s, 0)),
        pl.BlockSpec((sb, WINDOW, KV_WIDTH), lambda b, s: (b, 0, 0)),
        pl.BlockSpec((sb, WINDOW, KV_WIDTH), lambda b, s: (b, 0, 0)),
        pl.BlockSpec((sb, SUBLANES, LRU_WIDTH), lambda b, s: (b, 0, 0)),
        pl.BlockSpec((sb, 1, LRU_WIDTH), lambda b, s: (b, 0, 0)),
    )
    scratch = [
        pltpu.VMEM((WINDOW + tile, KV_WIDTH), F32),
        pltpu.VMEM((WINDOW + tile, KV_WIDTH), F32),
        pltpu.VMEM((SUBLANES + tile, LRU_WIDTH), F32),
        pltpu.VMEM((1, LRU_WIDTH), F32),
        pltpu.VMEM((3, tile, D_MODEL), F32),
    ]
    kern = functools.partial(_mixer_kernel, tile=tile, chunk=chunk, hist0=hist0, meta_valid=meta_valid,
                             streams=streams)
    return pl.pallas_call(
        kern,
        grid=(n_b, seq // tile),
        in_specs=in_specs,
        out_specs=out_specs,
        out_shape=out_shape,
        scratch_shapes=scratch,
        compiler_params=pltpu.CompilerParams(
            dimension_semantics=("arbitrary", "arbitrary"), vmem_limit_bytes=VMEM_LIMIT_BYTES),
        name=f"mixer_t{tile}_c{chunk}_s{streams}",
    )(wts["sinks"], x, cos, sin, kwin0, vwin0, conv0, h0, kmeta, vmeta, wts["g_mix"], wts["w_in"],
      wts["w_attn_o"], wts["w_rec_o"], wts["w_out"], wts["conv_w"], wts["conv_b"], wts["w_gate"],
      wts["b_r"], wts["b_i"], wts["lam"])


def _split_bf16(v, parts):
    out = []
    for _ in range(parts):
        p = v.astype(BF16)
        out.append(p)
        v = v - p.astype(F32)
    return out


def _channel_kernel(h_ref, gffn_ref, wr_ref, wg_ref, wu_ref, wd_ref, gfin_ref, y_ref, xs, cs, acc, *, tile, rows):
    h = h_ref[...]
    nb = _rmsnorm(h, gffn_ref[...]).astype(BF16)
    logits = lax.dot_general(wr_ref[...], nb, (((1,), (1,)), ((), ())), preferred_element_type=F32)
    neg = -jnp.inf
    big = float(ROUTER_ROWS)

    gl = logits[N_EXPERTS:N_EXPERTS + N_GROUPS, :]
    g_id = lax.broadcasted_iota(jnp.int32, gl.shape, 0).astype(F32)
    gmax = jnp.max(gl, axis=0, keepdims=True)
    g_row = jnp.min(jnp.where(gl == gmax, g_id, big), axis=0, keepdims=True)
    g_w = 1.0 / jnp.sum(jnp.exp(gl - gmax), axis=0, keepdims=True)

    e_id = lax.broadcasted_iota(jnp.int32, (N_EXPERTS, tile), 0)
    e_idf = e_id.astype(F32)
    emask = (e_id // EXPERTS_PER_GROUP).astype(F32) == g_row
    el = jnp.where(emask, logits[0:N_EXPERTS, :], neg)
    v1 = jnp.max(el, axis=0, keepdims=True)
    i1 = jnp.min(jnp.where(el == v1, e_idf, big), axis=0, keepdims=True)
    el2 = jnp.where(e_idf == i1, neg, el)
    v2 = jnp.max(el2, axis=0, keepdims=True)
    i2 = jnp.min(jnp.where(el2 == v2, e_idf, big), axis=0, keepdims=True)
    e2 = jnp.exp(v2 - v1)
    den = 1.0 + e2
    w1 = (1.0 / den) * g_w
    w2 = (e2 / den) * g_w
    comb = jnp.where(e_idf == i1, w1, 0.0) + jnp.where(e_idf == i2, w2, 0.0)

    g_col = jnp.transpose(jnp.broadcast_to(g_row, (LANES, tile)))[:, 0:1]
    t_col = lax.broadcasted_iota(jnp.int32, (tile, 1), 0)
    t_row = lax.broadcasted_iota(jnp.int32, (1, tile), 1)
    a_before_b = (g_col < g_row) | ((g_col == g_row) & (t_col < t_row))
    pos_row = jnp.sum(a_before_b.astype(F32), axis=0, keepdims=True)
    pos_col = jnp.transpose(jnp.broadcast_to(pos_row, (LANES, tile)))[:, 0:1]
    perm = (pos_row == t_col.astype(F32)).astype(BF16)
    perm_t = (pos_col == t_row.astype(F32)).astype(BF16)

    xs[...] = _dot(perm, nb).astype(BF16)
    comb_rows = jnp.concatenate(
        _split_bf16(comb, COMB_PARTS) + [jnp.zeros((ROUTER_LANES - COMB_PARTS * N_EXPERTS, tile), BF16)], axis=0)
    cs[...] = lax.dot_general(perm, comb_rows, (((1,), (1,)), ((), ())), preferred_element_type=F32)

    start = [0.0] + [jnp.sum((g_row < float(g)).astype(F32)) for g in range(1, N_GROUPS)] + [float(tile)]

    acc[...] = jnp.zeros_like(acc)
    for c in range(tile // rows):
        blk = pl.ds(c * rows, rows)
        for g in range(N_GROUPS):
            @pl.when((start[g] < float((c + 1) * rows)) & (start[g + 1] > float(c * rows)))
            def _experts(blk=blk, g=g):
                xc = xs[blk, :]
                cc = cs[blk, :]
                experts = range(g * EXPERTS_PER_GROUP, (g + 1) * EXPERTS_PER_GROUP)
                hgs = [_dot(xc, wg_ref[e]) for e in experts]
                hus = [_dot(xc, wu_ref[e]) for e in experts]
                out = None
                for e, hg, hu in zip(experts, hgs, hus):
                    weight = cc[:, e:e + 1]
                    for part in range(1, COMB_PARTS):
                        weight = weight + cc[:, part * N_EXPERTS + e:part * N_EXPERTS + e + 1]
                    act = ((hg * _sigmoid(hg)) * hu) * weight
                    down = _dot(act.astype(BF16), wd_ref[e])
                    out = down if out is None else out + down
                acc[blk, :] += out

    moe_parts = _split_bf16(acc[...], 2)
    moe = _dot(perm_t, moe_parts[0]) + _dot(perm_t, moe_parts[1])
    y_ref[...] = _rmsnorm(h + moe, gfin_ref[...])


def _channel(h, wts, *, tile):
    n_tok = h.shape[0]
    rows = min(tile, EXPERT_ROWS)
    assert n_tok % tile == 0 and tile % rows == 0

    def const(shape):
        nd = len(shape)
        return pl.BlockSpec(shape, lambda i, _nd=nd: (0,) * _nd, pipeline_mode=pl.Buffered(1))

    return pl.pallas_call(
        functools.partial(_channel_kernel, tile=tile, rows=rows),
        grid=(n_tok // tile,),
        in_specs=[
            pl.BlockSpec((tile, D_MODEL), lambda i: (i, 0)),
            const((1, D_MODEL)),
            const((ROUTER_ROWS, D_MODEL)),
            const((N_EXPERTS, D_MODEL, D_EXPERT)),
            const((N_EXPERTS, D_MODEL, D_EXPERT)),
            const((N_EXPERTS, D_EXPERT, D_MODEL)),
            const((1, D_MODEL)),
        ],
        out_specs=pl.BlockSpec((tile, D_MODEL), lambda i: (i, 0)),
        out_shape=jax.ShapeDtypeStruct((n_tok, D_MODEL), F32),
        scratch_shapes=[
            pltpu.VMEM((tile, D_MODEL), BF16),
            pltpu.VMEM((tile, ROUTER_LANES), F32),
            pltpu.VMEM((tile, D_MODEL), F32),
        ],
        compiler_params=pltpu.CompilerParams(
            dimension_semantics=("arbitrary",), vmem_limit_bytes=VMEM_LIMIT_BYTES),
        name=f"channel_t{tile}",
    )(h, wts["g_ffn"], wts["w_router"], wts["w_gate_e"], wts["w_up_e"], wts["w_down_e"], wts["g_final"])


def _rope_tables(first, count):
    inv_freq = ROPE_THETA ** (-np.arange(0, HEAD_DIM, 2, dtype=np.float64) / HEAD_DIM)
    ang = np.arange(first, first + count, dtype=np.float64)[:, None] * inv_freq[None, :]
    cos = np.cos(ang).astype(np.float32)
    sin = np.sin(ang).astype(np.float32)
    reps = LANES // HEAD_DIM
    return (jnp.asarray(np.tile(np.concatenate([cos, cos], axis=1), (1, reps))),
            jnp.asarray(np.tile(np.concatenate([-sin, sin], axis=1), (1, reps))))


def _block_diag_gates(w_r, w_i):
    per = MXU_DIM // LRU_BLOCK_DIM

    def bd(w):
        w = w.reshape(GATE_BLOCKS, per, LRU_BLOCK_DIM, LRU_BLOCK_DIM)
        eye = jnp.eye(per, dtype=w.dtype)
        full = jnp.einsum("bpde,pq->bpdqe", w, eye)
        return full.reshape(GATE_BLOCKS, MXU_DIM, MXU_DIM)

    return jnp.concatenate([bd(w_r), bd(w_i)], axis=-1).astype(BF16)


def _pad_meta(t, chunk):
    return jnp.pad(t, ((0, KEYS_PAD - WINDOW - chunk - N_META), (0, 0)))


def kernel(x_prompt, x_sample, cache_attn_k, cache_attn_v, state_conv, state_lru, meta_tokens,
           norm_mix_g, w_in, attn_sinks, w_attn_o, conv_w, conv_b, w_rg_r, b_rg_r, w_rg_i, b_rg_i,
           lru_lambda, w_rec_o, w_out, norm_ffn_g, w_group_router, w_expert_router, w_gate_e, w_up_e,
           w_down_e, norm_final_g):
    n_b, seq, _ = x_prompt.shape
    n_db, dseq, _ = x_sample.shape
    past_len = 2048
    layer = 0
    row = lambda t: t.reshape(1, -1)
    w_router = jnp.concatenate([w_expert_router[layer], w_group_router[layer]], axis=1).T
    w_router = jnp.pad(w_router, ((0, ROUTER_ROWS - N_EXPERTS - N_GROUPS), (0, 0)))
    wts = dict(
        sinks=attn_sinks[layer], g_mix=row(norm_mix_g[layer]), w_in=w_in[layer].astype(BF16),
        w_attn_o=w_attn_o[layer].astype(BF16), w_rec_o=w_rec_o[layer].astype(BF16), w_out=w_out[layer].astype(BF16),
        conv_w=conv_w[layer], conv_b=row(conv_b[layer]), w_gate=_block_diag_gates(w_rg_r[layer], w_rg_i[layer]),
        b_r=row(b_rg_r[layer]), b_i=row(b_rg_i[layer]), lam=row(lru_lambda[layer]),
        g_ffn=row(norm_ffn_g[layer]), w_router=w_router.astype(BF16), w_gate_e=w_gate_e[layer].astype(BF16),
        w_up_e=w_up_e[layer].astype(BF16), w_down_e=w_down_e[layer].astype(BF16), g_final=row(norm_final_g),
    )

    zeros_win = jnp.zeros((1, WINDOW, KV_WIDTH), F32)
    zeros_conv = jnp.zeros((1, SUBLANES, LRU_WIDTH), F32)
    zeros_h = jnp.zeros((1, 1, LRU_WIDTH), F32)

    cos_m, sin_m = _rope_tables(0, N_META)
    zero_meta = jnp.zeros((KEYS_PAD - WINDOW - N_META, KV_WIDTH), F32)
    _, k_m, v_m, conv_m, h_m = _mixer(
        meta_tokens[None], cos_m, sin_m, zeros_win, zeros_win, zeros_conv, zeros_h, zero_meta, zero_meta, wts,
        tile=N_META, chunk=N_META, hist0=0, meta_valid=False)
    k_meta = k_m[0, WINDOW - N_META:]
    v_meta = v_m[0, WINDOW - N_META:]

    tile_p = 512 if seq % 512 == 0 else CHUNK
    cos_p, sin_p = _rope_tables(N_META, seq)
    h_p, k_p, v_p, conv_p, lru_p = _mixer(
        x_prompt, cos_p, sin_p, zeros_win, zeros_win, conv_m, h_m, _pad_meta(k_meta, CHUNK), _pad_meta(v_meta, CHUNK),
        wts, tile=tile_p, chunk=CHUNK, hist0=0, meta_valid=True)

    cos_s, sin_s = _rope_tables(N_META + past_len, dseq)
    conv_s0 = jnp.pad(state_conv[layer], ((0, 0), (SUBLANES - (CONV_WIDTH - 1), 0), (0, 0)))
    h_s, k_s, v_s, conv_s, lru_s = _mixer(
        x_sample.reshape(1, n_db * dseq, D_MODEL), jnp.tile(cos_s, (n_db, 1)), jnp.tile(sin_s, (n_db, 1)),
        cache_attn_k[layer].reshape(n_db, WINDOW, KV_WIDTH), cache_attn_v[layer].reshape(n_db, WINDOW, KV_WIDTH),
        conv_s0, state_lru[layer][:, None, :], _pad_meta(k_meta, dseq), _pad_meta(v_meta, dseq), wts,
        tile=n_db * dseq, chunk=dseq, hist0=WINDOW, meta_valid=True, streams=n_db)

    tok_p = n_b * seq
    tok_s = n_db * dseq
    y_p = _channel(h_p.reshape(tok_p, D_MODEL), wts, tile=512 if tok_p % 512 == 0 else CHUNK)
    y_s = _channel(h_s.reshape(tok_s, D_MODEL), wts, tile=tok_s)

    keep = SUBLANES - (CONV_WIDTH - 1)
    kv_shape = lambda t: t.reshape(1, t.shape[0], WINDOW, N_KV_HEADS, HEAD_DIM)
    return (y_p.reshape(n_b, seq, D_MODEL), y_s.reshape(n_db, dseq, D_MODEL),
            kv_shape(k_p), kv_shape(v_p), conv_p[None, :, keep:, :], lru_p.reshape(1, n_b, LRU_WIDTH),
            kv_shape(k_s), kv_shape(v_s), conv_s[None, :, keep:, :], lru_s.reshape(1, n_db, LRU_WIDTH))
```
